```python
import math
import jax
import jax.numpy as jnp
from jax import lax
import numpy as np

D_MODEL = 1024
BATCH = 8
SEQ = 8192
DEPTH = 1
DEC_BATCH = 32
DEC_SEQ = 64
PAST_LEN = 4096

CHUNK = 64
Q_BLOCK = 128
D_MIX = D_MODEL
D_ATTN = D_MIX // 2
D_SSM = D_MIX - D_ATTN
N_HEADS = 8
QK_NOPE = 64
QK_ROPE = 32
V_DIM = D_ATTN // N_HEADS
KV_LORA = 256
Q_LORA = 768
ROPE_THETA = 10000.0
SSM_GROUP = 16
N_SSM_GROUPS = D_SSM // SSM_GROUP
SSM_STATE = 64
D_FF = 4 * D_MODEL
D_IN = Q_LORA + KV_LORA + QK_ROPE + D_SSM
SPLITS = (Q_LORA, Q_LORA + KV_LORA, Q_LORA + KV_LORA + QK_ROPE)
SOFTMAX_SCALE = (QK_NOPE + QK_ROPE) ** -0.5
EPS = 1e-6
NEG_INF = -1e30

kernel_name = "hybrid_mla_s5_streaming_step"


def rmsnorm(x, g):
    xf = x.astype(jnp.float32)
    y = xf * lax.rsqrt(jnp.mean(xf * xf, axis=-1, keepdims=True) + EPS)
    return (y * g.astype(jnp.float32)).astype(x.dtype)


def apply_rope(x, pos):
    rdim = x.shape[-1]
    half = rdim // 2
    inv_freq = ROPE_THETA ** (-(jnp.arange(half, dtype=jnp.float32) * 2.0) / rdim)
    ang = pos.astype(jnp.float32)[:, None] * inv_freq[None, :]
    cos = jnp.cos(ang)[None, :, None, :]
    sin = jnp.sin(ang)[None, :, None, :]
    xf = x.astype(jnp.float32)
    x1, x2 = xf[..., :half], xf[..., half:]
    return jnp.concatenate([x1 * cos - x2 * sin, x1 * sin + x2 * cos], axis=-1).astype(x.dtype)


def chunk_attention(q, k, v, q_pos, k_pos):
    s = jnp.einsum("bqhd,bkhd->bhqk", q, k, preferred_element_type=jnp.float32) * SOFTMAX_SCALE
    visible = (k_pos[None, :] // CHUNK) <= (q_pos[:, None] // CHUNK)
    s = jnp.where(visible[None, None], s, NEG_INF)
    p = jax.nn.softmax(s, axis=-1)
    return jnp.einsum("bhqk,bkhd->bqhd", p.astype(v.dtype), v)


def mla_attention(q, k, v, q_pos, k_pos):
    bsz, t = q.shape[0], q.shape[1]
    if t <= Q_BLOCK:
        return chunk_attention(q, k, v, q_pos, k_pos)
    nb = t // Q_BLOCK
    qb = q.reshape(bsz, nb, Q_BLOCK, N_HEADS, q.shape[-1]).transpose(1, 0, 2, 3, 4)
    pb = q_pos.reshape(nb, Q_BLOCK)
    ob = lax.map(lambda args: chunk_attention(args[0], k, v, args[1], k_pos), (qb, pb))
    return ob.transpose(1, 0, 2, 3, 4).reshape(bsz, t, N_HEADS, V_DIM)


def s5_discretize(a_re, a_im, log_step):
    dt = jnp.exp(log_step.astype(jnp.float32))[:, None]
    lr, li = a_re.astype(jnp.float32), a_im.astype(jnp.float32)
    mag = jnp.exp(lr * dt)
    lb_re, lb_im = mag * jnp.cos(li * dt), mag * jnp.sin(li * dt)
    nr, ni = lb_re - 1.0, lb_im
    den = lr * lr + li * li
    coef_re = (nr * lr + ni * li) / den
    coef_im = (ni * lr - nr * li) / den
    return lb_re, lb_im, coef_re, coef_im


def _ssm_combine(e1, e2):
    a1r, a1i, b1r, b1i = e1
    a2r, a2i, b2r, b2i = e2
    return (a1r * a2r - a1i * a2i,
            a1r * a2i + a1i * a2r,
            a2r * b1r - a2i * b1i + b2r,
            a2r * b1i + a2i * b1r + b2i)


def s5_mixer(u, h0_re, h0_im, a_re, a_im, log_step, b_re, b_im, c_re, c_im, d_skip, w_glu):
    bsz, t, _ = u.shape
    f32 = jnp.float32
    lb_re, lb_im, coef_re, coef_im = s5_discretize(a_re, a_im, log_step)
    uf = u.astype(f32).reshape(bsz, t, N_SSM_GROUPS, SSM_GROUP)
    bu_re = jnp.einsum("btgp,gnp->btgn", uf, b_re.astype(f32))
    bu_im = jnp.einsum("btgp,gnp->btgn", uf, b_im.astype(f32))
    x_re = coef_re * bu_re - coef_im * bu_im
    x_im = coef_re * bu_im + coef_im * bu_re
    h0r, h0i = h0_re.astype(f32), h0_im.astype(f32)
    x_re = x_re.at[:, 0].add(lb_re * h0r - lb_im * h0i)
    x_im = x_im.at[:, 0].add(lb_re * h0i + lb_im * h0r)
    shape = (1, t, N_SSM_GROUPS, SSM_STATE)
    a_r = jnp.broadcast_to(lb_re, shape)
    a_i = jnp.broadcast_to(lb_im, shape)
    _, _, h_re, h_im = lax.associative_scan(_ssm_combine, (a_r, a_i, x_re, x_im), axis=1)
    y = (jnp.einsum("btgn,gpn->btgp", h_re, c_re.astype(f32))
         - jnp.einsum("btgn,gpn->btgp", h_im, c_im.astype(f32))
         + d_skip.astype(f32).reshape(N_SSM_GROUPS, SSM_GROUP) * uf)
    y = jax.nn.gelu(y.reshape(bsz, t, D_SSM))
    y = y * jax.nn.sigmoid(y @ w_glu.astype(f32))
    return y.astype(u.dtype), h_re[:, -1].astype(h0_re.dtype), h_im[:, -1].astype(h0_im.dtype)


def hybrid_layer(x, pos, past_latent, past_k_rope, past_pos, h0_re, h0_im,
                 g_mix, w_in, g_q_a, w_q_up, g_kv_a, w_kv_up, a_re, a_im, log_step,
                 b_re, b_im, c_re, c_im, d_skip, w_glu, g_attn_out, g_ssm_out, w_out,
                 g_mlp, w_up, w_down):
    bsz, t, _ = x.shape
    xn = rmsnorm(x, g_mix)
    c_q, c_kv, k_pe, u = jnp.split(xn @ w_in, SPLITS, axis=-1)
    q = (rmsnorm(c_q, g_q_a) @ w_q_up).reshape(bsz, t, N_HEADS, QK_NOPE + QK_ROPE)
    q = jnp.concatenate([q[..., :QK_NOPE], apply_rope(q[..., QK_NOPE:], pos)], axis=-1)
    latent = rmsnorm(c_kv, g_kv_a)
    k_rope = apply_rope(k_pe[:, :, None, :], pos)[:, :, 0, :]
    if past_latent is None:
        all_latent, all_k_rope, k_pos = latent, k_rope, pos
    else:
        all_latent = jnp.concatenate([past_latent, latent], axis=1)
        all_k_rope = jnp.concatenate([past_k_rope, k_rope], axis=1)
        k_pos = jnp.concatenate([past_pos, pos], axis=0)
    tk = all_latent.shape[1]
    kv = (all_latent @ w_kv_up).reshape(bsz, tk, N_HEADS, QK_NOPE + V_DIM)
    k = jnp.concatenate(
        [kv[..., :QK_NOPE], jnp.broadcast_to(all_k_rope[:, :, None, :], (bsz, tk, N_HEADS, QK_ROPE))],
        axis=-1)
    v = kv[..., QK_NOPE:]
    attn = mla_attention(q, k, v, pos, k_pos).reshape(bsz, t, D_ATTN)
    ssm, h_re, h_im = s5_mixer(u, h0_re, h0_im, a_re, a_im, log_step,
                               b_re, b_im, c_re, c_im, d_skip, w_glu)
    mixed = jnp.concatenate([rmsnorm(attn, g_attn_out), rmsnorm(ssm, g_ssm_out)], axis=-1) @ w_out
    h = x + mixed
    hn = rmsnorm(h, g_mlp)
    h = h + jnp.square(jax.nn.relu(hn @ w_up)) @ w_down
    return h, latent, k_rope, h_re, h_im


def setup_inputs(seed: int = 0) -> dict:
    key = jax.random.key(seed)
    ks = jax.random.split(key, 32)
    f32 = jnp.float32
    nrm = lambda k, shape, scale: jax.random.normal(k, shape, f32) * scale
    gain = lambda k, shape: 1.0 + 0.01 * jax.random.normal(k, shape, f32)
    L, G, N, P = DEPTH, N_SSM_GROUPS, SSM_STATE, SSM_GROUP
    n_idx = jnp.arange(N, dtype=f32)
    a_re = -0.5 + 0.01 * jax.random.normal(ks[6], (L, G, N), f32)
    a_im = math.pi * n_idx[None, None, :] + 0.01 * jax.random.normal(ks[7], (L, G, N), f32)
    log_step = jax.random.uniform(ks[8], (L, G), f32, math.log(1e-3), math.log(1e-1))
    return {
        "x_prompt": nrm(ks[0], (BATCH, SEQ, D_MODEL), 1.0),
        "x_sample": nrm(ks[1], (DEC_BATCH, DEC_SEQ, D_MODEL), 1.0),
        "cache_kv_latent": nrm(ks[2], (L, DEC_BATCH, PAST_LEN, KV_LORA), 1.0),
        "cache_k_rope": nrm(ks[3], (L, DEC_BATCH, PAST_LEN, QK_ROPE), 1.0),
        "state_ssm_re": nrm(ks[4], (L, DEC_BATCH, G, N), 0.5),
        "state_ssm_im": nrm(ks[5], (L, DEC_BATCH, G, N), 0.5),
        "g_mix": gain(ks[9], (L, D_MODEL)),
        "w_in": nrm(ks[10], (L, D_MODEL, D_IN), D_MODEL ** -0.5),
        "g_q_a": gain(ks[11], (L, Q_LORA)),
        "w_q_up": nrm(ks[12], (L, Q_LORA, N_HEADS * (QK_NOPE + QK_ROPE)), Q_LORA ** -0.5),
        "g_kv_a": gain(ks[13], (L, KV_LORA)),
        "w_kv_up": nrm(ks[14], (L, KV_LORA, N_HEADS * (QK_NOPE + V_DIM)), KV_LORA ** -0.5),
        "a_re": a_re,
        "a_im": a_im,
        "log_step": log_step,
        "b_re": nrm(ks[15], (L, G, N, P), (2 * P) ** -0.5),
        "b_im": nrm(ks[16], (L, G, N, P), (2 * P) ** -0.5),
        "c_re": nrm(ks[17], (L, G, P, N), N ** -0.5),
        "c_im": nrm(ks[18], (L, G, P, N), N ** -0.5),
        "d_skip": nrm(ks[19], (L, D_SSM), 1.0),
        "w_glu": nrm(ks[20], (L, D_SSM, D_SSM), D_SSM ** -0.5),
        "g_attn_out": gain(ks[21], (L, D_ATTN)),
        "g_ssm_out": gain(ks[22], (L, D_SSM)),
        "w_out": nrm(ks[23], (L, D_MIX, D_MODEL), D_MIX ** -0.5),
        "g_mlp": gain(ks[24], (L, D_MODEL)),
        "w_up": nrm(ks[25], (L, D_MODEL, D_FF), D_MODEL ** -0.5),
        "w_down": nrm(ks[26], (L, D_FF, D_MODEL), D_FF ** -0.5),
        "g_final": gain(ks[27], (D_MODEL,)),
    }


def reference(x_prompt, x_sample, cache_kv_latent, cache_k_rope, state_ssm_re, state_ssm_im,
              g_mix, w_in, g_q_a, w_q_up, g_kv_a, w_kv_up, a_re, a_im, log_step,
              b_re, b_im, c_re, c_im, d_skip, w_glu, g_attn_out, g_ssm_out, w_out,
              g_mlp, w_up, w_down, g_final):
    layer_weights = (g_mix, w_in, g_q_a, w_q_up, g_kv_a, w_kv_up, a_re, a_im, log_step,
                     b_re, b_im, c_re, c_im, d_skip, w_glu, g_attn_out, g_ssm_out, w_out,
                     g_mlp, w_up, w_down)

    def trunk(x, pos, past_latent, past_k_rope, past_pos, h0_re, h0_im):
        lat_out, kr_out, hr_out, hi_out = [], [], [], []
        for layer in range(DEPTH):
            w_l = [w[layer] for w in layer_weights]
            x, lat, kr, hr, hi = hybrid_layer(
                x, pos,
                None if past_latent is None else past_latent[layer],
                None if past_k_rope is None else past_k_rope[layer],
                past_pos, h0_re[layer], h0_im[layer], *w_l)
            lat_out.append(lat)
            kr_out.append(kr)
            hr_out.append(hr)
            hi_out.append(hi)
        return (rmsnorm(x, g_final), jnp.stack(lat_out), jnp.stack(kr_out),
                jnp.stack(hr_out), jnp.stack(hi_out))

    bp, tp = x_prompt.shape[0], x_prompt.shape[1]
    pos_p = jnp.arange(tp, dtype=jnp.int32)
    h0p = jnp.zeros((DEPTH, bp, N_SSM_GROUPS, SSM_STATE), state_ssm_re.dtype)
    y_prompt, lat_p, kr_p, hr_p, hi_p = trunk(x_prompt, pos_p, None, None, None, h0p, h0p)

    past = cache_kv_latent.shape[2]
    ts = x_sample.shape[1]
    past_pos = jnp.arange(past, dtype=jnp.int32)
    pos_s = past + jnp.arange(ts, dtype=jnp.int32)
    y_sample, lat_s, kr_s, hr_s, hi_s = trunk(x_sample, pos_s, cache_kv_latent, cache_k_rope,
                                              past_pos, state_ssm_re, state_ssm_im)
    return (y_prompt, y_sample, lat_p, kr_p, hr_p, hi_p, lat_s, kr_s, hr_s, hi_s)
```

```python
import functools
import math

import jax
import jax.numpy as jnp
from jax import lax
from jax.experimental import pallas as pl
from jax.experimental.pallas import tpu as pltpu

F32 = jnp.float32
BF16 = jnp.bfloat16

D_MODEL = 1024
N_HEADS = 8
QK_NOPE = 64
QK_ROPE = 32
ROPE_HALF = QK_ROPE // 2
V_DIM = 64
KV_LORA = 256
Q_LORA = 768
D_ATTN = N_HEADS * V_DIM
D_SSM = 512
SSM_GROUP = 16
N_GROUPS = D_SSM // SSM_GROUP
SSM_STATE = 64
N_STATE = N_GROUPS * SSM_STATE
D_FF = 4 * D_MODEL
CHUNK = 64
ROPE_THETA = 10000.0
SOFTMAX_SCALE = (QK_NOPE + QK_ROPE) ** -0.5
EPS = 1e-6
NEG_INF = -1e30

LANES = 128
SUBLANES = 8
HEAD_PAD = LANES
D_HEADS_PAD = N_HEADS * HEAD_PAD
COL_CQ = 0
COL_CKV = Q_LORA
COL_U = Q_LORA + KV_LORA
COL_KPE = COL_U + D_SSM
COL_KPE_ROT = COL_KPE + LANES
D_IN_EXT = COL_KPE_ROT + LANES

VMEM_LIMIT = 56 * 1024 * 1024

PROJ_TM = 512
ATTN_TQ = 256
ATTN_TK_SAMPLE = 512
SSM_TT = 64
POST_TM = 512
FF_CHUNK = 1024


def _const_spec(shape):
    nd = len(shape)
    return pl.BlockSpec(shape, lambda *_: (0,) * nd, pipeline_mode=pl.Buffered(1))


def _rms(x, g):
    return x * lax.rsqrt(jnp.mean(x * x, axis=-1, keepdims=True) + EPS) * g


def _prep_kernel(are_ref, aim_ref, ls_ref, bre_ref, bim_ref, lre_ref, lim_ref, bbre_ref, bbim_ref):
    dt = jnp.exp(ls_ref[...])
    lr, li = are_ref[...], aim_ref[...]
    mag = jnp.exp(lr * dt)
    lb_re, lb_im = mag * jnp.cos(li * dt), mag * jnp.sin(li * dt)
    nr, ni = lb_re - 1.0, lb_im
    den = lr * lr + li * li
    coef_re = (nr * lr + ni * li) / den
    coef_im = (ni * lr - nr * li) / den
    lre_ref[...] = lb_re
    lim_ref[...] = lb_im
    bre, bim = bre_ref[...], bim_ref[...]
    cr, ci = coef_re[:, None, :], coef_im[:, None, :]
    bbre_ref[...] = cr * bre - ci * bim
    bbim_ref[...] = cr * bim + ci * bre


def _ssm_prep(a_re, a_im, log_step, b_re, b_im):
    g, n = a_re.shape
    p = b_re.shape[-1]
    bre_t = jnp.swapaxes(b_re, 1, 2)
    bim_t = jnp.swapaxes(b_im, 1, 2)
    return pl.pallas_call(
        _prep_kernel,
        out_shape=(jax.ShapeDtypeStruct((g, n), F32), jax.ShapeDtypeStruct((g, n), F32),
                   jax.ShapeDtypeStruct((g, p, n), F32), jax.ShapeDtypeStruct((g, p, n), F32)),
        name="ssm_prep",
    )(a_re, a_im, log_step.reshape(g, 1), bre_t, bim_t)


def _proj_kernel(x_ref, cq_ref, sq_ref, ck_ref, sk_ref, gmix_ref, win_ref, gq_ref, wq_ref, gkv_ref,
                 wkv_ref, e_ref, vones_ref,
                 q_out, lat_out, kr_out, kpad_out, vpad_out, u_out):
    xn = _rms(x_ref[...], gmix_ref[...]).astype(BF16)
    proj = jnp.dot(xn, win_ref[...], preferred_element_type=F32)
    u_out[...] = proj[:, COL_U:COL_U + D_SSM]

    cqn = _rms(proj[:, COL_CQ:COL_CQ + Q_LORA], gq_ref[...]).astype(BF16)
    qq = jnp.dot(cqn, wq_ref[...], preferred_element_type=F32)
    cq_t, sq_t = cq_ref[...], sq_ref[...]
    for h in range(N_HEADS):
        lo = h * HEAD_PAD
        q_h = qq[:, lo:lo + HEAD_PAD] * cq_t + qq[:, D_HEADS_PAD + lo:D_HEADS_PAD + lo + HEAD_PAD] * sq_t
        q_out[:, lo:lo + HEAD_PAD] = q_h.astype(BF16)

    lat = _rms(proj[:, COL_CKV:COL_CKV + KV_LORA], gkv_ref[...])
    lat_out[...] = lat
    kr = (proj[:, COL_KPE:COL_KPE + LANES] * ck_ref[...]
          + proj[:, COL_KPE_ROT:COL_KPE_ROT + LANES] * sk_ref[...])
    kr_out[...] = kr[:, :QK_ROPE]
    kv = jnp.dot(lat.astype(BF16), wkv_ref[...], preferred_element_type=F32)
    kpad = kv[:, :D_HEADS_PAD] + jnp.dot(kr.astype(BF16), e_ref[...], preferred_element_type=F32)
    kpad_out[...] = kpad.astype(BF16)
    vpad_out[...] = (kv[:, D_HEADS_PAD:] + vones_ref[...]).astype(BF16)


def _proj(x2d, tabs, w, tm):
    n = x2d.shape[0]
    cosq, sinq, cosk, sink = tabs
    n_tab = cosq.shape[0] // tm
    row = lambda i: (i, 0)
    tab = lambda i: (i % n_tab, 0)
    out_shape = (
        jax.ShapeDtypeStruct((n, D_HEADS_PAD), BF16),
        jax.ShapeDtypeStruct((n, KV_LORA), F32),
        jax.ShapeDtypeStruct((n, QK_ROPE), F32),
        jax.ShapeDtypeStruct((n, D_HEADS_PAD), BF16),
        jax.ShapeDtypeStruct((n, D_HEADS_PAD), BF16),
        jax.ShapeDtypeStruct((n, D_SSM), F32),
    )
    return pl.pallas_call(
        _proj_kernel,
        grid=(n // tm,),
        in_specs=[
            pl.BlockSpec((tm, D_MODEL), row),
            pl.BlockSpec((tm, LANES), tab), pl.BlockSpec((tm, LANES), tab),
            pl.BlockSpec((tm, LANES), tab), pl.BlockSpec((tm, LANES), tab),
            _const_spec((1, D_MODEL)), _const_spec((D_MODEL, D_IN_EXT)),
            _const_spec((1, Q_LORA)), _const_spec((Q_LORA, 2 * D_HEADS_PAD)),
            _const_spec((1, KV_LORA)), _const_spec((KV_LORA, 2 * D_HEADS_PAD)),
            _const_spec((LANES, D_HEADS_PAD)), _const_spec((1, D_HEADS_PAD)),
        ],
        out_specs=(
            pl.BlockSpec((tm, D_HEADS_PAD), row), pl.BlockSpec((tm, KV_LORA), row),
            pl.BlockSpec((tm, QK_ROPE), row), pl.BlockSpec((tm, D_HEADS_PAD), row),
            pl.BlockSpec((tm, D_HEADS_PAD), row), pl.BlockSpec((tm, D_SSM), row),
        ),
        out_shape=out_shape,
        compiler_params=pltpu.CompilerParams(dimension_semantics=("parallel",),
                                             vmem_limit_bytes=VMEM_LIMIT),
        name="proj",
    )(x2d, cosq, sinq, cosk, sink, w["g_mix"], w["w_in_ext"], w["g_q_a"], w["w_q2"], w["g_kv_a"],
      w["w_kv2"], w["e_rope"], w["v_ones"])


def _decomp_kernel(lat_ref, kr_ref, wkv_ref, e_ref, vones_ref, kpad_out, vpad_out):
    kv = jnp.dot(lat_ref[...].astype(BF16), wkv_ref[...], preferred_element_type=F32)
    kpad = kv[:, :D_HEADS_PAD] + jnp.dot(kr_ref[...], e_ref[...], preferred_element_type=F32)
    kpad_out[...] = kpad.astype(BF16)
    vpad_out[...] = (kv[:, D_HEADS_PAD:] + vones_ref[...]).astype(BF16)


def _decompress(lat2d, kr2d_pad, w, tm):
    n = lat2d.shape[0]
    row = lambda i: (i, 0)
    return pl.pallas_call(
        _decomp_kernel,
        grid=(n // tm,),
        in_specs=[pl.BlockSpec((tm, KV_LORA), row), pl.BlockSpec((tm, LANES), row),
                  _const_spec((KV_LORA, 2 * D_HEADS_PAD)), _const_spec((LANES, D_HEADS_PAD)),
                  _const_spec((1, D_HEADS_PAD))],
        out_specs=(pl.BlockSpec((tm, D_HEADS_PAD), row), pl.BlockSpec((tm, D_HEADS_PAD), row)),
        out_shape=(jax.ShapeDtypeStruct((n, D_HEADS_PAD), BF16),
                   jax.ShapeDtypeStruct((n, D_HEADS_PAD), BF16)),
        compiler_params=pltpu.CompilerParams(dimension_semantics=("parallel",),
                                             vmem_limit_bytes=VMEM_LIMIT),
        name="kv_decompress",
    )(lat2d, kr2d_pad, w["w_kv2"], w["e_rope"], w["v_ones"])


def _softmax_step(qh, kt, vt, m, acc, mask):
    s = lax.dot_general(qh, kt, (((1,), (1,)), ((), ())), preferred_element_type=F32)
    if mask is not None:
        s = jnp.where(mask, s, NEG_INF)
    m_new = jnp.maximum(m, jnp.max(s, axis=-1, keepdims=True))
    alpha = jnp.exp(m - m_new)
    p = jnp.exp(s - m_new)
    acc = acc * alpha + jnp.dot(p.astype(BF16), vt, preferred_element_type=F32)
    return m_new, acc


def _merge_head_pair(acc0, acc1):
    lane = lax.broadcasted_iota(jnp.int32, acc0.shape, 1)
    lo = lane < V_DIM
    num = jnp.where(lo, acc0, pltpu.roll(acc1, V_DIM, 1))
    den = jnp.where(lo, pltpu.roll(acc0, V_DIM, 1), acc1)
    return num / den


def _attn_prompt_kernel(q_ref, k_ref, v_ref, o_ref, *, tq):
    i = pl.program_id(2)
    rows = lax.broadcasted_iota(jnp.int32, (tq, tq), 0) // CHUNK
    cols = lax.broadcasted_iota(jnp.int32, (tq, tq), 1) // CHUNK
    diag_mask = cols <= rows
    accs = []
    for h in range(2):
        sl = slice(h * HEAD_PAD, (h + 1) * HEAD_PAD)
        qh = q_ref[:, sl]

        def body(j, carry, sl=sl, qh=qh):
            off = pl.multiple_of(j * tq, tq)
            return _softmax_step(qh, k_ref[pl.ds(off, tq), sl], v_ref[pl.ds(off, tq), sl], *carry, None)

        m0 = jnp.full((tq, 1), NEG_INF, F32)
        acc0 = jnp.zeros((tq, HEAD_PAD), F32)
        m, acc = lax.fori_loop(0, i, body, (m0, acc0))
        off = pl.multiple_of(i * tq, tq)
        _, acc = _softmax_step(qh, k_ref[pl.ds(off, tq), sl], v_ref[pl.ds(off, tq), sl], m, acc, diag_mask)
        accs.append(acc)
    o_ref[...] = _merge_head_pair(*accs)


def _attn_prompt(q, kpad, vpad, tq):
    b, t, _ = q.shape
    assert tq % CHUNK == 0 and t % tq == 0
    return pl.pallas_call(
        functools.partial(_attn_prompt_kernel, tq=tq),
        grid=(b, N_HEADS // 2, t // tq),
        in_specs=[pl.BlockSpec((None, tq, 2 * HEAD_PAD), lambda bi, hp, i: (bi, i, hp)),
                  pl.BlockSpec((None, t, 2 * HEAD_PAD), lambda bi, hp, i: (bi, 0, hp)),
                  pl.BlockSpec((None, t, 2 * HEAD_PAD), lambda bi, hp, i: (bi, 0, hp))],
        out_specs=pl.BlockSpec((None, tq, 2 * V_DIM), lambda bi, hp, i: (bi, i, hp)),
        out_shape=jax.ShapeDtypeStruct((b, t, D_ATTN), F32),
        compiler_params=pltpu.CompilerParams(
            dimension_semantics=("parallel", "parallel", "arbitrary"), vmem_limit_bytes=VMEM_LIMIT),
        name="attn_prompt",
    )(q, kpad, vpad)


def _attn_sample_kernel(q_ref, kp_ref, vp_ref, kn_ref, vn_ref, o_ref, *, tk):
    tq = q_ref.shape[0]
    n_past = kp_ref.shape[0] // tk
    accs = []
    for h in range(2):
        sl = slice(h * HEAD_PAD, (h + 1) * HEAD_PAD)
        qh = q_ref[:, sl]

        def body(j, carry, sl=sl, qh=qh):
            off = pl.multiple_of(j * tk, tk)
            return _softmax_step(qh, kp_ref[pl.ds(off, tk), sl], vp_ref[pl.ds(off, tk), sl], *carry, None)

        m0 = jnp.full((tq, 1), NEG_INF, F32)
        acc0 = jnp.zeros((tq, HEAD_PAD), F32)
        m, acc = lax.fori_loop(0, n_past, body, (m0, acc0))
        _, acc = _softmax_step(qh, kn_ref[:, sl], vn_ref[:, sl], m, acc, None)
        accs.append(acc)
    o_ref[...] = _merge_head_pair(*accs)


def _attn_sample(q, kpast, vpast, knew, vnew, tk):
    b, tq, _ = q.shape
    past = kpast.shape[1]
    assert past % tk == 0
    new_blk = lambda bi, hp: (bi, 0, hp)
    return pl.pallas_call(
        functools.partial(_attn_sample_kernel, tk=tk),
        grid=(b, N_HEADS // 2),
        in_specs=[pl.BlockSpec((None, tq, 2 * HEAD_PAD), new_blk),
                  pl.BlockSpec((None, past, 2 * HEAD_PAD), new_blk),
                  pl.BlockSpec((None, past, 2 * HEAD_PAD), new_blk),
                  pl.BlockSpec((None, tq, 2 * HEAD_PAD), new_blk),
                  pl.BlockSpec((None, tq, 2 * HEAD_PAD), new_blk)],
        out_specs=pl.BlockSpec((None, tq, 2 * V_DIM), new_blk),
        out_shape=jax.ShapeDtypeStruct((b, tq, D_ATTN), F32),
        compiler_params=pltpu.CompilerParams(dimension_semantics=("parallel", "parallel"),
                                             vmem_limit_bytes=VMEM_LIMIT),
        name="attn_sample",
    )(q, kpast, vpast, knew, vnew)


SCAN_LANES = 512


def _ssm_kernel(u_ref, h0_ref, lre_ref, lim_ref, bblk_ref, cblk_ref, d_ref, wglu_ref, gssm_ref,
                y_ref, hout_ref, x_s, h_s, hc_s, *, tt):
    ti = pl.program_id(1)

    @pl.when(ti == 0)
    def _():
        hc_s[...] = h0_ref[...]

    rows = tt * SUBLANES
    u = u_ref[...].reshape(rows, D_SSM)
    x_s[...] = jnp.dot(u.astype(BF16), bblk_ref[...],
                       preferred_element_type=F32).reshape(tt, SUBLANES, 2 * N_STATE)

    for c in range(N_STATE // SCAN_LANES):
        re = slice(c * SCAN_LANES, (c + 1) * SCAN_LANES)
        im = slice(N_STATE + c * SCAN_LANES, N_STATE + (c + 1) * SCAN_LANES)
        lr = jnp.broadcast_to(lre_ref[:, re], (SUBLANES, SCAN_LANES))
        li = jnp.broadcast_to(lim_ref[:, re], (SUBLANES, SCAN_LANES))

        def step(t, carry, re=re, im=im, lr=lr, li=li):
            hr, hi = carry
            nhr = lr * hr - li * hi + x_s[t, :, re]
            nhi = lr * hi + li * hr + x_s[t, :, im]
            h_s[t, :, re] = nhr
            h_s[t, :, im] = nhi
            return nhr, nhi

        hr, hi = lax.fori_loop(0, tt, step, (hc_s[:, re], hc_s[:, im]), unroll=8)
        hc_s[:, re] = hr
        hc_s[:, im] = hi

    hb = h_s[...].reshape(rows, 2 * N_STATE).astype(BF16)
    y = jnp.dot(hb, cblk_ref[...], preferred_element_type=F32) + d_ref[...] * u
    y = jax.nn.gelu(y, approximate=True)
    z = jnp.dot(y.astype(BF16), wglu_ref[...], preferred_element_type=F32)
    y = y * (1.0 / (1.0 + jnp.exp(-z)))
    y_ref[...] = _rms(y, gssm_ref[...]).reshape(tt, SUBLANES, D_SSM)

    @pl.when(ti == pl.num_programs(1) - 1)
    def _():
        hout_ref[...] = hc_s[...]


def _ssm(u_tb, h0, w, tt):
    t, nb, _ = u_tb.shape
    assert nb % SUBLANES == 0 and t % tt == 0
    return pl.pallas_call(
        functools.partial(_ssm_kernel, tt=tt),
        grid=(nb // SUBLANES, t // tt),
        in_specs=[pl.BlockSpec((tt, SUBLANES, D_SSM), lambda bg, ti: (ti, bg, 0)),
                  pl.BlockSpec((SUBLANES, 2 * N_STATE), lambda bg, ti: (bg, 0)),
                  _const_spec((1, N_STATE)), _const_spec((1, N_STATE)),
                  _const_spec((D_SSM, 2 * N_STATE)), _const_spec((2 * N_STATE, D_SSM)),
                  _const_spec((1, D_SSM)), _const_spec((D_SSM, D_SSM)), _const_spec((1, D_SSM))],
        out_specs=(pl.BlockSpec((tt, SUBLANES, D_SSM), lambda bg, ti: (ti, bg, 0)),
                   pl.BlockSpec((SUBLANES, 2 * N_STATE), lambda bg, ti: (bg, 0))),
        out_shape=(jax.ShapeDtypeStruct((t, nb, D_SSM), F32),
                   jax.ShapeDtypeStruct((nb, 2 * N_STATE), F32)),
        scratch_shapes=[pltpu.VMEM((tt, SUBLANES, 2 * N_STATE), F32),
                        pltpu.VMEM((tt, SUBLANES, 2 * N_STATE), F32),
                        pltpu.VMEM((SUBLANES, 2 * N_STATE), F32)],
        compiler_params=pltpu.CompilerParams(dimension_semantics=("parallel", "arbitrary"),
                                             vmem_limit_bytes=VMEM_LIMIT),
        name="ssm",
    )(u_tb, h0, w["lam_re"], w["lam_im"], w["b_blk"], w["c_blk"], w["d_skip"], w["w_glu"], w["g_ssm_out"])


def _post_kernel(x_ref, attn_ref, ssm_ref, gattn_ref, wouta_ref, wouts_ref, gmlp_ref, wup_ref, wdown_ref,
                 gfin_ref, y_ref):
    an = _rms(attn_ref[...], gattn_ref[...]).astype(BF16)
    mixed = (jnp.dot(an, wouta_ref[...], preferred_element_type=F32)
             + jnp.dot(ssm_ref[...].astype(BF16), wouts_ref[...], preferred_element_type=F32))
    h = x_ref[...] + mixed
    hn = _rms(h, gmlp_ref[...]).astype(BF16)
    acc = jnp.zeros(h.shape, F32)
    for c in range(D_FF // FF_CHUNK):
        ff = slice(c * FF_CHUNK, (c + 1) * FF_CHUNK)
        a = jnp.dot(hn, wup_ref[:, ff], preferred_element_type=F32)
        a = jnp.square(jnp.maximum(a, 0.0))
        acc = acc + jnp.dot(a.astype(BF16), wdown_ref[ff, :], preferred_element_type=F32)
    y_ref[...] = _rms(h + acc, gfin_ref[...])


def _post(x2d, attn2d, ssm2d, w, tm):
    n = x2d.shape[0]
    row = lambda i: (i, 0)
    return pl.pallas_call(
        _post_kernel,
        grid=(n // tm,),
        in_specs=[pl.BlockSpec((tm, D_MODEL), row), pl.BlockSpec((tm, D_ATTN), row),
                  pl.BlockSpec((tm, D_SSM), row),
                  _const_spec((1, D_ATTN)), _const_spec((D_ATTN, D_MODEL)), _const_spec((D_SSM, D_MODEL)),
                  _const_spec((1, D_MODEL)), _const_spec((D_MODEL, D_FF)), _const_spec((D_FF, D_MODEL)),
                  _const_spec((1, D_MODEL))],
        out_specs=pl.BlockSpec((tm, D_MODEL), row),
        out_shape=jax.ShapeDtypeStruct((n, D_MODEL), F32),
        compiler_params=pltpu.CompilerParams(dimension_semantics=("parallel",),
                                             vmem_limit_bytes=VMEM_LIMIT),
        name="post",
    )(x2d, attn2d, ssm2d, w["g_attn_out"], w["w_out_attn"], w["w_out_ssm"], w["g_mlp"], w["w_up"],
      w["w_down"], w["g_final"])


def _rope_tables(pos):
    t = pos.shape[0]
    inv_freq = ROPE_THETA ** (-(jnp.arange(ROPE_HALF, dtype=F32) * 2.0) / QK_ROPE)
    ang = pos.astype(F32)[:, None] * inv_freq[None, :]
    cc = jnp.tile(jnp.cos(ang), (1, 2))
    ss = jnp.tile(jnp.sin(ang), (1, 2))
    pad_hi = jnp.zeros((t, HEAD_PAD - QK_NOPE - QK_ROPE), F32)
    cosq = jnp.concatenate([jnp.full((t, QK_NOPE), SOFTMAX_SCALE, F32), cc * SOFTMAX_SCALE, pad_hi], axis=1)
    sinq = jnp.concatenate([jnp.zeros((t, QK_NOPE), F32), ss * SOFTMAX_SCALE, pad_hi], axis=1)
    pad_k = jnp.zeros((t, LANES - QK_ROPE), F32)
    cosk = jnp.concatenate([cc, pad_k], axis=1)
    sink = jnp.concatenate([ss, pad_k], axis=1)
    return cosq, sinq, cosk, sink


def _rot_cols(w_x1, w_x2):
    return -w_x2, w_x1


def _layer_weights(g_mix, w_in, g_q_a, w_q_up, g_kv_a, w_kv_up, a_re, a_im, log_step, b_re, b_im,
                   c_re, c_im, d_skip, w_glu, g_attn_out, g_ssm_out, w_out, g_mlp, w_up, w_down, g_final):
    w = {}
    w["g_mix"] = g_mix[None, :]
    w_cq, w_ckv = w_in[:, :Q_LORA], w_in[:, Q_LORA:Q_LORA + KV_LORA]
    w_kpe = w_in[:, Q_LORA + KV_LORA:Q_LORA + KV_LORA + QK_ROPE]
    w_u = w_in[:, Q_LORA + KV_LORA + QK_ROPE:]
    rot1, rot2 = _rot_cols(w_kpe[:, :ROPE_HALF], w_kpe[:, ROPE_HALF:])
    zk = jnp.zeros((D_MODEL, LANES - QK_ROPE), F32)
    w["w_in_ext"] = jnp.concatenate([w_cq, w_ckv, w_u, w_kpe, zk, rot1, rot2, zk], axis=1).astype(BF16)

    w["g_q_a"] = g_q_a[None, :]
    wq = w_q_up.reshape(Q_LORA, N_HEADS, QK_NOPE + QK_ROPE)
    nope, r1, r2 = wq[:, :, :QK_NOPE], wq[:, :, QK_NOPE:QK_NOPE + ROPE_HALF], wq[:, :, QK_NOPE + ROPE_HALF:]
    zq = jnp.zeros((Q_LORA, N_HEADS, HEAD_PAD - QK_NOPE - QK_ROPE), F32)
    rot1, rot2 = _rot_cols(r1, r2)
    wq_pad = jnp.concatenate([nope, r1, r2, zq], axis=2).reshape(Q_LORA, D_HEADS_PAD)
    wq_rot = jnp.concatenate([jnp.zeros_like(nope), rot1, rot2, zq], axis=2).reshape(Q_LORA, D_HEADS_PAD)
    w["w_q2"] = jnp.concatenate([wq_pad, wq_rot], axis=1).astype(BF16)

    w["g_kv_a"] = g_kv_a[None, :]
    wkv = w_kv_up.reshape(KV_LORA, N_HEADS, QK_NOPE + V_DIM)
    zkv = jnp.zeros((KV_LORA, N_HEADS, HEAD_PAD - QK_NOPE), F32)
    wk_pad = jnp.concatenate([wkv[:, :, :QK_NOPE], zkv], axis=2).reshape(KV_LORA, D_HEADS_PAD)
    wv_pad = jnp.concatenate([wkv[:, :, QK_NOPE:], zkv], axis=2).reshape(KV_LORA, D_HEADS_PAD)
    w["w_kv2"] = jnp.concatenate([wk_pad, wv_pad], axis=1).astype(BF16)
    place = jnp.zeros((LANES, HEAD_PAD), F32).at[jnp.arange(QK_ROPE), QK_NOPE + jnp.arange(QK_ROPE)].set(1.0)
    w["e_rope"] = jnp.tile(place, (1, N_HEADS)).astype(BF16)
    w["v_ones"] = jnp.tile(jnp.concatenate([jnp.zeros((V_DIM,), F32), jnp.ones((HEAD_PAD - V_DIM,), F32)]),
                           N_HEADS)[None, :]

    lam_re, lam_im, bb_re, bb_im = _ssm_prep(a_re, a_im, log_step, b_re, b_im)
    w["lam_re"] = lam_re.reshape(1, N_STATE)
    w["lam_im"] = lam_im.reshape(1, N_STATE)
    eye = jnp.eye(N_GROUPS, dtype=F32)
    blk = lambda m: jnp.einsum("gpn,gh->gphn", m, eye).reshape(D_SSM, N_STATE)
    w["b_blk"] = jnp.concatenate([blk(bb_re), blk(bb_im)], axis=1).astype(BF16)
    blk_t = lambda m: jnp.einsum("gpn,gh->gnhp", m, eye).reshape(N_STATE, D_SSM)
    w["c_blk"] = jnp.concatenate([blk_t(c_re), blk_t(-c_im)], axis=0).astype(BF16)
    w["d_skip"] = d_skip[None, :]
    w["w_glu"] = w_glu.astype(BF16)
    w["g_ssm_out"] = g_ssm_out[None, :]

    w["g_attn_out"] = g_attn_out[None, :]
    w["w_out_attn"] = w_out[:D_ATTN].astype(BF16)
    w["w_out_ssm"] = w_out[D_ATTN:].astype(BF16)
    w["g_mlp"] = g_mlp[None, :]
    w["w_up"] = w_up.astype(BF16)
    w["w_down"] = w_down.astype(BF16)
    w["g_final"] = g_final[None, :]
    return w


def _pack_state(h_re, h_im):
    nb = h_re.shape[0]
    return jnp.concatenate([h_re.reshape(nb, N_STATE), h_im.reshape(nb, N_STATE)], axis=1)


def _unpack_state(h):
    nb = h.shape[0]
    return (h[:, :N_STATE].reshape(nb, N_GROUPS, SSM_STATE), h[:, N_STATE:].reshape(nb, N_GROUPS, SSM_STATE))


def _branch(x, pos, past, h0, w, *, proj_tm, post_tm, ssm_tt):
    b, t, _ = x.shape
    n = b * t
    x2d = x.reshape(n, D_MODEL)
    tabs = _rope_tables(pos)
    if t < proj_tm:
        assert proj_tm % t == 0
        tabs = tuple(jnp.tile(a, (proj_tm // t, 1)) for a in tabs)
    else:
        assert t % proj_tm == 0
    q, lat, kr, kpad, vpad, u = _proj(x2d, tabs, w, proj_tm)
    q3, k3, v3 = (a.reshape(b, t, D_HEADS_PAD) for a in (q, kpad, vpad))
    if past is None:
        attn = _attn_prompt(q3, k3, v3, min(ATTN_TQ, t))
    else:
        past_lat, past_kr = past
        plen = past_lat.shape[1]
        assert plen % CHUNK == 0 and t <= CHUNK
        kr_pad = jnp.pad(past_kr.reshape(b * plen, QK_ROPE), ((0, 0), (0, LANES - QK_ROPE))).astype(BF16)
        kp, vp = _decompress(past_lat.reshape(b * plen, KV_LORA), kr_pad, w, proj_tm)
        attn = _attn_sample(q3, kp.reshape(b, plen, D_HEADS_PAD), vp.reshape(b, plen, D_HEADS_PAD), k3, v3,
                            min(ATTN_TK_SAMPLE, plen))
    u_tb = jnp.swapaxes(u.reshape(b, t, D_SSM), 0, 1)
    y_tb, h_fin = _ssm(u_tb, h0, w, min(ssm_tt, t))
    ssm2d = jnp.swapaxes(y_tb, 0, 1).reshape(n, D_SSM)
    y = _post(x2d, attn.reshape(n, D_ATTN), ssm2d, w, post_tm)
    h_re, h_im = _unpack_state(h_fin)
    return (y.reshape(b, t, D_MODEL), lat.reshape(1, b, t, KV_LORA), kr.reshape(1, b, t, QK_ROPE),
            h_re[None], h_im[None])


def kernel(x_prompt, x_sample, cache_kv_latent, cache_k_rope, state_ssm_re, state_ssm_im, g_mix, w_in, g_q_a,
           w_q_up, g_kv_a, w_kv_up, a_re, a_im, log_step, b_re, b_im, c_re, c_im, d_skip, w_glu, g_attn_out,
           g_ssm_out, w_out, g_mlp, w_up, w_down, g_final):
    assert g_mix.shape[0] == 1, "single-layer trunk"
    w = _layer_weights(g_mix[0], w_in[0], g_q_a[0], w_q_up[0], g_kv_a[0], w_kv_up[0], a_re[0], a_im[0],
                       log_step[0], b_re[0], b_im[0], c_re[0], c_im[0], d_skip[0], w_glu[0], g_attn_out[0],
                       g_ssm_out[0], w_out[0], g_mlp[0], w_up[0], w_down[0], g_final)
    tiles = dict(proj_tm=PROJ_TM, post_tm=POST_TM, ssm_tt=SSM_TT)

    bp, tp, _ = x_prompt.shape
    pos_p = jnp.arange(tp, dtype=jnp.int32)
    h0p = jnp.zeros((bp, 2 * N_STATE), F32)
    y_p, lat_p, kr_p, hr_p, hi_p = _branch(x_prompt, pos_p, None, h0p, w, **tiles)

    bs, ts, _ = x_sample.shape
    plen = cache_kv_latent.shape[2]
    pos_s = plen + jnp.arange(ts, dtype=jnp.int32)
    h0s = _pack_state(state_ssm_re[0], state_ssm_im[0])
    y_s, lat_s, kr_s, hr_s, hi_s = _branch(x_sample, pos_s, (cache_kv_latent[0], cache_k_rope[0]), h0s, w, **tiles)
    return (y_p, y_s, lat_p, kr_p, hr_p, hi_p, lat_s, kr_s, hr_s, hi_s)
```

```python
import functools
import math

import jax
import jax.numpy as jnp
from jax import lax
from jax.experimental import pallas as pl
from jax.experimental.pallas import tpu as pltpu

F32 = jnp.float32
BF16 = jnp.bfloat16

D_MODEL = 1024
N_HEADS = 8
QK_NOPE = 64
QK_ROPE = 32
ROPE_HALF = QK_ROPE // 2
V_DIM = 64
KV_LORA = 256
Q_LORA = 768
D_ATTN = N_HEADS * V_DIM
D_SSM = 512
SSM_GROUP = 16
N_GROUPS = D_SSM // SSM_GROUP
SSM_STATE = 64
N_STATE = N_GROUPS * SSM_STATE
D_FF = 4 * D_MODEL
CHUNK = 64
ROPE_THETA = 10000.0
SOFTMAX_SCALE = (QK_NOPE + QK_ROPE) ** -0.5
LOG2_E = math.log2(math.e)
EPS = 1e-6
NEG_INF = -1e30

LANES = 128
SUBLANES = 8
HEAD_PAD = LANES
D_HEADS_PAD = N_HEADS * HEAD_PAD
COL_CQ = 0
COL_CKV = Q_LORA
COL_U = Q_LORA + KV_LORA
COL_KPE = COL_U + D_SSM
COL_KPE_ROT = COL_KPE + LANES
D_IN_EXT = COL_KPE_ROT + LANES

VMEM_LIMIT = 56 * 1024 * 1024

PROJ_TM = 512
ATTN_TQ = 512
ATTN_TK_SAMPLE = 512
SSM_TT = 64
POST_TM = 512
FF_CHUNK = 1024


def _const_spec(shape):
    nd = len(shape)
    return pl.BlockSpec(shape, lambda *_: (0,) * nd, pipeline_mode=pl.Buffered(1))


def _rms(x, g):
    return x * lax.rsqrt(jnp.mean(x * x, axis=-1, keepdims=True) + EPS) * g


def _prep_kernel(are_ref, aim_ref, ls_ref, bre_ref, bim_ref, lre_ref, lim_ref, bbre_ref, bbim_ref):
    dt = jnp.exp(ls_ref[...])
    lr, li = are_ref[...], aim_ref[...]
    mag = jnp.exp(lr * dt)
    lb_re, lb_im = mag * jnp.cos(li * dt), mag * jnp.sin(li * dt)
    nr, ni = lb_re - 1.0, lb_im
    den = lr * lr + li * li
    coef_re = (nr * lr + ni * li) / den
    coef_im = (ni * lr - nr * li) / den
    lre_ref[...] = lb_re
    lim_ref[...] = lb_im
    bre, bim = bre_ref[...], bim_ref[...]
    cr, ci = coef_re[:, None, :], coef_im[:, None, :]
    bbre_ref[...] = cr * bre - ci * bim
    bbim_ref[...] = cr * bim + ci * bre


def _ssm_prep(a_re, a_im, log_step, b_re, b_im):
    g, n = a_re.shape
    p = b_re.shape[-1]
    bre_t = jnp.swapaxes(b_re, 1, 2)
    bim_t = jnp.swapaxes(b_im, 1, 2)
    return pl.pallas_call(
        _prep_kernel,
        out_shape=(jax.ShapeDtypeStruct((g, n), F32), jax.ShapeDtypeStruct((g, n), F32),
                   jax.ShapeDtypeStruct((g, p, n), F32), jax.ShapeDtypeStruct((g, p, n), F32)),
        name="ssm_prep",
    )(a_re, a_im, log_step.reshape(g, 1), bre_t, bim_t)


def _proj_kernel(x_ref, cq_ref, sq_ref, ck_ref, sk_ref, gmix_ref, win_ref, gq_ref, wq_ref, gkv_ref,
                 wk_ref, wv_ref, e_ref,
                 q_out, lat_out, kr_out, kpad_out, v_out, u_out, *, v_key_tile):
    xn = _rms(x_ref[...], gmix_ref[...]).astype(BF16)
    proj = jnp.dot(xn, win_ref[...], preferred_element_type=F32)
    u_out[...] = proj[:, COL_U:COL_U + D_SSM]

    cqn = _rms(proj[:, COL_CQ:COL_CQ + Q_LORA], gq_ref[...]).astype(BF16)
    qq = jnp.dot(cqn, wq_ref[...], preferred_element_type=F32)
    cq_t, sq_t = cq_ref[...], sq_ref[...]
    for h in range(N_HEADS):
        lo = h * HEAD_PAD
        q_h = qq[:, lo:lo + HEAD_PAD] * cq_t + qq[:, D_HEADS_PAD + lo:D_HEADS_PAD + lo + HEAD_PAD] * sq_t
        q_out[:, lo:lo + HEAD_PAD] = q_h.astype(BF16)

    lat = _rms(proj[:, COL_CKV:COL_CKV + KV_LORA], gkv_ref[...])
    lat_out[...] = lat
    kr = (proj[:, COL_KPE:COL_KPE + LANES] * ck_ref[...]
          + proj[:, COL_KPE_ROT:COL_KPE_ROT + LANES] * sk_ref[...])
    kr_out[...] = kr[:, :QK_ROPE]
    latb = lat.astype(BF16)
    kpad = (jnp.dot(latb, wk_ref[...], preferred_element_type=F32)
            + jnp.dot(kr.astype(BF16), e_ref[...], preferred_element_type=F32))
    kpad_out[...] = kpad.astype(BF16)
    if v_key_tile is None:
        v = jnp.dot(latb, wv_ref[...], preferred_element_type=F32)
        ones = lax.broadcasted_iota(jnp.int32, v.shape, 1) % HEAD_PAD >= V_DIM
        v_out[...] = jnp.where(ones, 1.0, v).astype(BF16)
    else:
        vt = lax.dot_general(wv_ref[...], latb, (((1,), (1,)), ((), ())),
                             preferred_element_type=F32)
        ones = lax.broadcasted_iota(jnp.int32, vt.shape, 0) % HEAD_PAD >= V_DIM
        vt = jnp.where(ones, 1.0, vt).astype(BF16)
        for s in range(vt.shape[1] // v_key_tile):
            v_out[s] = vt[:, s * v_key_tile:(s + 1) * v_key_tile]


def _proj(x2d, tabs, w, tm, stream_len, v_key_tile):
    n = x2d.shape[0]
    cosq, sinq, cosk, sink = tabs
    n_tab = cosq.shape[0] // tm
    row = lambda i: (i, 0)
    tab = lambda i: (i % n_tab, 0)
    if v_key_tile is None:
        v_shape = jax.ShapeDtypeStruct((n, D_HEADS_PAD), BF16)
        v_spec = pl.BlockSpec((tm, D_HEADS_PAD), row)
        wv = w["w_v_pad"]
    else:
        assert stream_len % tm == 0 and tm % v_key_tile == 0
        tiles_per_stream = stream_len // tm
        v_shape = jax.ShapeDtypeStruct((n // stream_len, stream_len // v_key_tile, D_HEADS_PAD, v_key_tile), BF16)
        v_spec = pl.BlockSpec((None, tm // v_key_tile, D_HEADS_PAD, v_key_tile),
                              lambda i: (i // tiles_per_stream, i % tiles_per_stream, 0, 0))
        wv = w["w_v_t"]
    out_shape = (
        jax.ShapeDtypeStruct((n, D_HEADS_PAD), BF16),
        jax.ShapeDtypeStruct((n, KV_LORA), F32),
        jax.ShapeDtypeStruct((n, QK_ROPE), F32),
        jax.ShapeDtypeStruct((n, D_HEADS_PAD), BF16),
        v_shape,
        jax.ShapeDtypeStruct((n, D_SSM), F32),
    )
    return pl.pallas_call(
        functools.partial(_proj_kernel, v_key_tile=v_key_tile),
        grid=(n // tm,),
        in_specs=[
            pl.BlockSpec((tm, D_MODEL), row),
            pl.BlockSpec((tm, LANES), tab), pl.BlockSpec((tm, LANES), tab),
            pl.BlockSpec((tm, LANES), tab), pl.BlockSpec((tm, LANES), tab),
            _const_spec((1, D_MODEL)), _const_spec((D_MODEL, D_IN_EXT)),
            _const_spec((1, Q_LORA)), _const_spec((Q_LORA, 2 * D_HEADS_PAD)),
            _const_spec((1, KV_LORA)), _const_spec((KV_LORA, D_HEADS_PAD)), _const_spec(wv.shape),
            _const_spec((LANES, D_HEADS_PAD)),
        ],
        out_specs=(
            pl.BlockSpec((tm, D_HEADS_PAD), row), pl.BlockSpec((tm, KV_LORA), row),
            pl.BlockSpec((tm, QK_ROPE), row), pl.BlockSpec((tm, D_HEADS_PAD), row),
            v_spec, pl.BlockSpec((tm, D_SSM), row),
        ),
        out_shape=out_shape,
        compiler_params=pltpu.CompilerParams(dimension_semantics=("parallel",),
                                             vmem_limit_bytes=VMEM_LIMIT),
        name="proj",
    )(x2d, cosq, sinq, cosk, sink, w["g_mix"], w["w_in_ext"], w["g_q_a"], w["w_q2"], w["g_kv_a"],
      w["w_k_pad"], wv, w["e_rope"])


def _decomp_kernel(lat_ref, kr_ref, wk_ref, wv_ref, e_ref, kpad_out, vpad_out):
    latb = lat_ref[...].astype(BF16)
    kpad = (jnp.dot(latb, wk_ref[...], preferred_element_type=F32)
            + jnp.dot(kr_ref[...], e_ref[...], preferred_element_type=F32))
    kpad_out[...] = kpad.astype(BF16)
    v = jnp.dot(latb, wv_ref[...], preferred_element_type=F32)
    ones = lax.broadcasted_iota(jnp.int32, v.shape, 1) % HEAD_PAD >= V_DIM
    vpad_out[...] = jnp.where(ones, 1.0, v).astype(BF16)


def _decompress(lat2d, kr2d_pad, w, tm):
    n = lat2d.shape[0]
    row = lambda i: (i, 0)
    return pl.pallas_call(
        _decomp_kernel,
        grid=(n // tm,),
        in_specs=[pl.BlockSpec((tm, KV_LORA), row), pl.BlockSpec((tm, LANES), row),
                  _const_spec((KV_LORA, D_HEADS_PAD)), _const_spec((KV_LORA, D_HEADS_PAD)),
                  _const_spec((LANES, D_HEADS_PAD))],
        out_specs=(pl.BlockSpec((tm, D_HEADS_PAD), row), pl.BlockSpec((tm, D_HEADS_PAD), row)),
        out_shape=(jax.ShapeDtypeStruct((n, D_HEADS_PAD), BF16),
                   jax.ShapeDtypeStruct((n, D_HEADS_PAD), BF16)),
        compiler_params=pltpu.CompilerParams(dimension_semantics=("parallel",),
                                             vmem_limit_bytes=VMEM_LIMIT),
        name="kv_decompress",
    )(lat2d, kr2d_pad, w["w_k_pad"], w["w_v_pad"], w["e_rope"])


def _softmax_step(qh, kt, vt, m, acc, mask):
    s = lax.dot_general(qh, kt, (((1,), (1,)), ((), ())), preferred_element_type=F32)
    if mask is not None:
        s = jnp.where(mask, s, NEG_INF)
    m_new = jnp.maximum(m, jnp.max(s, axis=-1, keepdims=True))
    alpha = jnp.exp2(m - m_new)
    p = jnp.exp2(s - m_new)
    acc = acc * alpha + jnp.dot(p.astype(BF16), vt, preferred_element_type=F32)
    return m_new, acc


def _merge_head_pair(acc0, acc1):
    lane = lax.broadcasted_iota(jnp.int32, acc0.shape, 1)
    lo = lane < V_DIM
    num = jnp.where(lo, acc0, pltpu.roll(acc1, V_DIM, 1))
    den = jnp.where(lo, pltpu.roll(acc0, V_DIM, 1), acc1)
    return num / den


def _attn_prompt_kernel(q_ref, k_ref, vt_ref, o_ref, *, tq):
    i = pl.program_id(2)
    key_chunk = lax.broadcasted_iota(jnp.int32, (tq, tq), 0) // CHUNK
    query_chunk = lax.broadcasted_iota(jnp.int32, (tq, tq), 1) // CHUNK
    diag_mask = key_chunk <= query_chunk
    heads = [slice(h * HEAD_PAD, (h + 1) * HEAD_PAD) for h in range(2)]

    def tile(j, carry, mask):
        off = pl.multiple_of(j * tq, tq)
        sts = [lax.dot_general(k_ref[pl.ds(off, tq), sl], q_ref[:, sl], (((1,), (1,)), ((), ())),
                               preferred_element_type=F32) for sl in heads]
        if mask is not None:
            sts = [jnp.where(mask, st, NEG_INF) for st in sts]
        m_new = [jnp.maximum(m, jnp.max(st, axis=0, keepdims=True)) for st, (m, _) in zip(sts, carry)]
        alphas = [jnp.exp2(m - mn) for mn, (m, _) in zip(m_new, carry)]
        ps = [jnp.exp2(st - mn).astype(BF16) for st, mn in zip(sts, m_new)]
        accs = [acc * a + jnp.dot(vt_ref[j, sl, :], p, preferred_element_type=F32)
                for sl, p, a, (_, acc) in zip(heads, ps, alphas, carry)]
        return tuple(zip(m_new, accs))

    init = tuple((jnp.full((1, tq), NEG_INF, F32), jnp.zeros((HEAD_PAD, tq), F32)) for _ in heads)
    carry = lax.fori_loop(0, i, lambda j, c: tile(j, c, None), init)
    carry = tile(i, carry, diag_mask)
    o_t = jnp.concatenate([acc[:V_DIM] / acc[V_DIM:V_DIM + 1] for _, acc in carry], axis=0)
    o_ref[...] = o_t.T


def _attn_prompt(q, kpad, vt, tq):
    b, t, _ = q.shape
    assert tq % CHUNK == 0 and t % tq == 0 and vt.shape == (b, t // tq, D_HEADS_PAD, tq)
    return pl.pallas_call(
        functools.partial(_attn_prompt_kernel, tq=tq),
        grid=(b, N_HEADS // 2, t // tq),
        in_specs=[pl.BlockSpec((None, tq, 2 * HEAD_PAD), lambda bi, hp, i: (bi, i, hp)),
                  pl.BlockSpec((None, t, 2 * HEAD_PAD), lambda bi, hp, i: (bi, 0, hp)),
                  pl.BlockSpec((None, t // tq, 2 * HEAD_PAD, tq), lambda bi, hp, i: (bi, 0, hp, 0))],
        out_specs=pl.BlockSpec((None, tq, 2 * V_DIM), lambda bi, hp, i: (bi, i, hp)),
        out_shape=jax.ShapeDtypeStruct((b, t, D_ATTN), F32),
        compiler_params=pltpu.CompilerParams(
            dimension_semantics=("parallel", "parallel", "arbitrary"), vmem_limit_bytes=VMEM_LIMIT),
        name="attn_prompt",
    )(q, kpad, vt)


def _attn_sample_kernel(q_ref, kp_ref, vp_ref, kn_ref, vn_ref, o_ref, *, tk):
    tq = q_ref.shape[0]
    n_past = kp_ref.shape[0] // tk
    accs = []
    for h in range(2):
        sl = slice(h * HEAD_PAD, (h + 1) * HEAD_PAD)
        qh = q_ref[:, sl]

        def body(j, carry, sl=sl, qh=qh):
            off = pl.multiple_of(j * tk, tk)
            return _softmax_step(qh, kp_ref[pl.ds(off, tk), sl], vp_ref[pl.ds(off, tk), sl], *carry, None)

        m0 = jnp.full((tq, 1), NEG_INF, F32)
        acc0 = jnp.zeros((tq, HEAD_PAD), F32)
        m, acc = lax.fori_loop(0, n_past, body, (m0, acc0))
        _, acc = _softmax_step(qh, kn_ref[:, sl], vn_ref[:, sl], m, acc, None)
        accs.append(acc)
    o_ref[...] = _merge_head_pair(*accs)


def _attn_sample(q, kpast, vpast, knew, vnew, tk):
    b, tq, _ = q.shape
    past = kpast.shape[1]
    assert past % tk == 0
    new_blk = lambda bi, hp: (bi, 0, hp)
    return pl.pallas_call(
        functools.partial(_attn_sample_kernel, tk=tk),
        grid=(b, N_HEADS // 2),
        in_specs=[pl.BlockSpec((None, tq, 2 * HEAD_PAD), new_blk),
                  pl.BlockSpec((None, past, 2 * HEAD_PAD), new_blk),
                  pl.BlockSpec((None, past, 2 * HEAD_PAD), new_blk),
                  pl.BlockSpec((None, tq, 2 * HEAD_PAD), new_blk),
                  pl.BlockSpec((None, tq, 2 * HEAD_PAD), new_blk)],
        out_specs=pl.BlockSpec((None, tq, 2 * V_DIM), new_blk),
        out_shape=jax.ShapeDtypeStruct((b, tq, D_ATTN), F32),
        compiler_params=pltpu.CompilerParams(dimension_semantics=("parallel", "parallel"),
                                             vmem_limit_bytes=VMEM_LIMIT),
        name="attn_sample",
    )(q, kpast, vpast, knew, vnew)


SCAN_LANES = 512


def _ssm_kernel(u_ref, h0_ref, lre_ref, lim_ref, bblk_ref, cblk_ref, d_ref, wglu_ref, gssm_ref,
                y_ref, hout_ref, x_s, h_s, hc_s, *, tt):
    ti = pl.program_id(1)

    @pl.when(ti == 0)
    def _():
        hc_s[...] = h0_ref[...]

    rows = tt * SUBLANES
    u = u_ref[...].reshape(rows, D_SSM)
    x_s[...] = jnp.dot(u.astype(BF16), bblk_ref[...],
                       preferred_element_type=F32).reshape(tt, SUBLANES, 2 * N_STATE)

    for c in range(N_STATE // SCAN_LANES):
        re = slice(c * SCAN_LANES, (c + 1) * SCAN_LANES)
        im = slice(N_STATE + c * SCAN_LANES, N_STATE + (c + 1) * SCAN_LANES)
        lr = jnp.broadcast_to(lre_ref[:, re], (SUBLANES, SCAN_LANES))
        li = jnp.broadcast_to(lim_ref[:, re], (SUBLANES, SCAN_LANES))

        def step(t, carry, re=re, im=im, lr=lr, li=li):
            hr, hi = carry
            nhr = lr * hr - li * hi + x_s[t, :, re]
            nhi = lr * hi + li * hr + x_s[t, :, im]
            h_s[t, :, re] = nhr
            h_s[t, :, im] = nhi
            return nhr, nhi

        hr, hi = lax.fori_loop(0, tt, step, (hc_s[:, re], hc_s[:, im]), unroll=8)
        hc_s[:, re] = hr
        hc_s[:, im] = hi

    hb = h_s[...].reshape(rows, 2 * N_STATE).astype(BF16)
    y = jnp.dot(hb, cblk_ref[...], preferred_element_type=F32) + d_ref[...] * u
    y = jax.nn.gelu(y, approximate=True)
    z = jnp.dot(y.astype(BF16), wglu_ref[...], preferred_element_type=F32)
    y = y * (1.0 / (1.0 + jnp.exp(-z)))
    y_ref[...] = _rms(y, gssm_ref[...]).reshape(tt, SUBLANES, D_SSM)

    @pl.when(ti == pl.num_programs(1) - 1)
    def _():
        hout_ref[...] = hc_s[...]


def _ssm(u_tb, h0, w, tt):
    t, nb, _ = u_tb.shape
    assert nb % SUBLANES == 0 and t % tt == 0
    return pl.pallas_call(
        functools.partial(_ssm_kernel, tt=tt),
        grid=(nb // SUBLANES, t // tt),
        in_specs=[pl.BlockSpec((tt, SUBLANES, D_SSM), lambda bg, ti: (ti, bg, 0)),
                  pl.BlockSpec((SUBLANES, 2 * N_STATE), lambda bg, ti: (bg, 0)),
                  _const_spec((1, N_STATE)), _const_spec((1, N_STATE)),
                  _const_spec((D_SSM, 2 * N_STATE)), _const_spec((2 * N_STATE, D_SSM)),
                  _const_spec((1, D_SSM)), _const_spec((D_SSM, D_SSM)), _const_spec((1, D_SSM))],
        out_specs=(pl.BlockSpec((tt, SUBLANES, D_SSM), lambda bg, ti: (ti, bg, 0)),
                   pl.BlockSpec((SUBLANES, 2 * N_STATE), lambda bg, ti: (bg, 0))),
        out_shape=(jax.ShapeDtypeStruct((t, nb, D_SSM), F32),
                   jax.ShapeDtypeStruct((nb, 2 * N_STATE), F32)),
        scratch_shapes=[pltpu.VMEM((tt, SUBLANES, 2 * N_STATE), F32),
                        pltpu.VMEM((tt, SUBLANES, 2 * N_STATE), F32),
                        pltpu.VMEM((SUBLANES, 2 * N_STATE), F32)],
        compiler_params=pltpu.CompilerParams(dimension_semantics=("parallel", "arbitrary"),
                                             vmem_limit_bytes=VMEM_LIMIT),
        name="ssm",
    )(u_tb, h0, w["lam_re"], w["lam_im"], w["b_blk"], w["c_blk"], w["d_skip"], w["w_glu"], w["g_ssm_out"])


def _post_kernel(x_ref, attn_ref, ssm_ref, gattn_ref, wouta_ref, wouts_ref, gmlp_ref, wup_ref, wdown_ref,
                 gfin_ref, y_ref):
    an = _rms(attn_ref[...], gattn_ref[...]).astype(BF16)
    mixed = (jnp.dot(an, wouta_ref[...], preferred_element_type=F32)
             + jnp.dot(ssm_ref[...].astype(BF16), wouts_ref[...], preferred_element_type=F32))
    h = x_ref[...] + mixed
    hn = _rms(h, gmlp_ref[...]).astype(BF16)
    acc = jnp.zeros(h.shape, F32)
    for c in range(D_FF // FF_CHUNK):
        ff = slice(c * FF_CHUNK, (c + 1) * FF_CHUNK)
        a = jnp.dot(hn, wup_ref[:, ff], preferred_element_type=F32)
        a = jnp.square(jnp.maximum(a, 0.0))
        acc = acc + jnp.dot(a.astype(BF16), wdown_ref[ff, :], preferred_element_type=F32)
    y_ref[...] = _rms(h + acc, gfin_ref[...])


def _post(x2d, attn2d, ssm2d, w, tm):
    n = x2d.shape[0]
    row = lambda i: (i, 0)
    return pl.pallas_call(
        _post_kernel,
        grid=(n // tm,),
        in_specs=[pl.BlockSpec((tm, D_MODEL), row), pl.BlockSpec((tm, D_ATTN), row),
                  pl.BlockSpec((tm, D_SSM), row),
                  _const_spec((1, D_ATTN)), _const_spec((D_ATTN, D_MODEL)), _const_spec((D_SSM, D_MODEL)),
                  _const_spec((1, D_MODEL)), _const_spec((D_MODEL, D_FF)), _const_spec((D_FF, D_MODEL)),
                  _const_spec((1, D_MODEL))],
        out_specs=pl.BlockSpec((tm, D_MODEL), row),
        out_shape=jax.ShapeDtypeStruct((n, D_MODEL), F32),
        compiler_params=pltpu.CompilerParams(dimension_semantics=("parallel",),
                                             vmem_limit_bytes=VMEM_LIMIT),
        name="post",
    )(x2d, attn2d, ssm2d, w["g_attn_out"], w["w_out_attn"], w["w_out_ssm"], w["g_mlp"], w["w_up"],
      w["w_down"], w["g_final"])


def _rope_tables(pos):
    t = pos.shape[0]
    inv_freq = ROPE_THETA ** (-(jnp.arange(ROPE_HALF, dtype=F32) * 2.0) / QK_ROPE)
    ang = pos.astype(F32)[:, None] * inv_freq[None, :]
    cc = jnp.tile(jnp.cos(ang), (1, 2))
    ss = jnp.tile(jnp.sin(ang), (1, 2))
    pad_hi = jnp.zeros((t, HEAD_PAD - QK_NOPE - QK_ROPE), F32)
    qs = SOFTMAX_SCALE * LOG2_E
    cosq = jnp.concatenate([jnp.full((t, QK_NOPE), qs, F32), cc * qs, pad_hi], axis=1)
    sinq = jnp.concatenate([jnp.zeros((t, QK_NOPE), F32), ss * qs, pad_hi], axis=1)
    pad_k = jnp.zeros((t, LANES - QK_ROPE), F32)
    cosk = jnp.concatenate([cc, pad_k], axis=1)
    sink = jnp.concatenate([ss, pad_k], axis=1)
    return cosq, sinq, cosk, sink


def _rot_cols(w_x1, w_x2):
    return -w_x2, w_x1


def _layer_weights(g_mix, w_in, g_q_a, w_q_up, g_kv_a, w_kv_up, a_re, a_im, log_step, b_re, b_im,
                   c_re, c_im, d_skip, w_glu, g_attn_out, g_ssm_out, w_out, g_mlp, w_up, w_down, g_final):
    w = {}
    w["g_mix"] = g_mix[None, :]
    w_cq, w_ckv = w_in[:, :Q_LORA], w_in[:, Q_LORA:Q_LORA + KV_LORA]
    w_kpe = w_in[:, Q_LORA + KV_LORA:Q_LORA + KV_LORA + QK_ROPE]
    w_u = w_in[:, Q_LORA + KV_LORA + QK_ROPE:]
    rot1, rot2 = _rot_cols(w_kpe[:, :ROPE_HALF], w_kpe[:, ROPE_HALF:])
    zk = jnp.zeros((D_MODEL, LANES - QK_ROPE), F32)
    w["w_in_ext"] = jnp.concatenate([w_cq, w_ckv, w_u, w_kpe, zk, rot1, rot2, zk], axis=1).astype(BF16)

    w["g_q_a"] = g_q_a[None, :]
    wq = w_q_up.reshape(Q_LORA, N_HEADS, QK_NOPE + QK_ROPE)
    nope, r1, r2 = wq[:, :, :QK_NOPE], wq[:, :, QK_NOPE:QK_NOPE + ROPE_HALF], wq[:, :, QK_NOPE + ROPE_HALF:]
    zq = jnp.zeros((Q_LORA, N_HEADS, HEAD_PAD - QK_NOPE - QK_ROPE), F32)
    rot1, rot2 = _rot_cols(r1, r2)
    wq_pad = jnp.concatenate([nope, r1, r2, zq], axis=2).reshape(Q_LORA, D_HEADS_PAD)
    wq_rot = jnp.concatenate([jnp.zeros_like(nope), rot1, rot2, zq], axis=2).reshape(Q_LORA, D_HEADS_PAD)
    w["w_q2"] = jnp.concatenate([wq_pad, wq_rot], axis=1).astype(BF16)

    w["g_kv_a"] = g_kv_a[None, :]
    wkv = w_kv_up.reshape(KV_LORA, N_HEADS, QK_NOPE + V_DIM)
    zkv = jnp.zeros((KV_LORA, N_HEADS, HEAD_PAD - QK_NOPE), F32)
    wk_pad = jnp.concatenate([wkv[:, :, :QK_NOPE], zkv], axis=2).reshape(KV_LORA, D_HEADS_PAD)
    wv_pad = jnp.concatenate([wkv[:, :, QK_NOPE:], zkv], axis=2).reshape(KV_LORA, D_HEADS_PAD)
    w["w_k_pad"] = wk_pad.astype(BF16)
    w["w_v_pad"] = wv_pad.astype(BF16)
    w["w_v_t"] = wv_pad.T.astype(BF16)
    place = jnp.zeros((LANES, HEAD_PAD), F32).at[jnp.arange(QK_ROPE), QK_NOPE + jnp.arange(QK_ROPE)].set(1.0)
    w["e_rope"] = jnp.tile(place, (1, N_HEADS)).astype(BF16)

    lam_re, lam_im, bb_re, bb_im = _ssm_prep(a_re, a_im, log_step, b_re, b_im)
    w["lam_re"] = lam_re.reshape(1, N_STATE)
    w["lam_im"] = lam_im.reshape(1, N_STATE)
    eye = jnp.eye(N_GROUPS, dtype=F32)
    blk = lambda m: jnp.einsum("gpn,gh->gphn", m, eye).reshape(D_SSM, N_STATE)
    w["b_blk"] = jnp.concatenate([blk(bb_re), blk(bb_im)], axis=1).astype(BF16)
    blk_t = lambda m: jnp.einsum("gpn,gh->gnhp", m, eye).reshape(N_STATE, D_SSM)
    w["c_blk"] = jnp.concatenate([blk_t(c_re), blk_t(-c_im)], axis=0).astype(BF16)
    w["d_skip"] = d_skip[None, :]
    w["w_glu"] = w_glu.astype(BF16)
    w["g_ssm_out"] = g_ssm_out[None, :]

    w["g_attn_out"] = g_attn_out[None, :]
    w["w_out_attn"] = w_out[:D_ATTN].astype(BF16)
    w["w_out_ssm"] = w_out[D_ATTN:].astype(BF16)
    w["g_mlp"] = g_mlp[None, :]
    w["w_up"] = w_up.astype(BF16)
    w["w_down"] = w_down.astype(BF16)
    w["g_final"] = g_final[None, :]
    return w


def _pack_state(h_re, h_im):
    nb = h_re.shape[0]
    return jnp.concatenate([h_re.reshape(nb, N_STATE), h_im.reshape(nb, N_STATE)], axis=1)


def _unpack_state(h):
    nb = h.shape[0]
    return (h[:, :N_STATE].reshape(nb, N_GROUPS, SSM_STATE), h[:, N_STATE:].reshape(nb, N_GROUPS, SSM_STATE))


def _branch(x, pos, past, h0, w, *, proj_tm, post_tm, ssm_tt):
    b, t, _ = x.shape
    n = b * t
    x2d = x.reshape(n, D_MODEL)
    tabs = _rope_tables(pos)
    if t < proj_tm:
        assert proj_tm % t == 0
        tabs = tuple(jnp.tile(a, (proj_tm // t, 1)) for a in tabs)
    else:
        assert t % proj_tm == 0
    tq = min(ATTN_TQ, t)
    q, lat, kr, kpad, v, u = _proj(x2d, tabs, w, proj_tm, t, tq if past is None else None)
    q3, k3 = q.reshape(b, t, D_HEADS_PAD), kpad.reshape(b, t, D_HEADS_PAD)
    if past is None:
        attn = _attn_prompt(q3, k3, v, tq)
    else:
        v3 = v.reshape(b, t, D_HEADS_PAD)
        past_lat, past_kr = past
        plen = past_lat.shape[1]
        assert plen % CHUNK == 0 and t <= CHUNK
        kr_pad = jnp.pad(past_kr.reshape(b * plen, QK_ROPE), ((0, 0), (0, LANES - QK_ROPE))).astype(BF16)
        kp, vp = _decompress(past_lat.reshape(b * plen, KV_LORA), kr_pad, w, proj_tm)
        attn = _attn_sample(q3, kp.reshape(b, plen, D_HEADS_PAD), vp.reshape(b, plen, D_HEADS_PAD), k3, v3,
                            min(ATTN_TK_SAMPLE, plen))
    u_tb = jnp.swapaxes(u.reshape(b, t, D_SSM), 0, 1)
    y_tb, h_fin = _ssm(u_tb, h0, w, min(ssm_tt, t))
    ssm2d = jnp.swapaxes(y_tb, 0, 1).reshape(n, D_SSM)
    y = _post(x2d, attn.reshape(n, D_ATTN), ssm2d, w, post_tm)
    h_re, h_im = _unpack_state(h_fin)
    return (y.reshape(b, t, D_MODEL), lat.reshape(1, b, t, KV_LORA), kr.reshape(1, b, t, QK_ROPE),
            h_re[None], h_im[None])


def kernel(x_prompt, x_sample, cache_kv_latent, cache_k_rope, state_ssm_re, state_ssm_im, g_mix, w_in, g_q_a,
           w_q_up, g_kv_a, w_kv_up, a_re, a_im, log_step, b_re, b_im, c_re, c_im, d_skip, w_glu, g_attn_out,
           g_ssm_out, w_out, g_mlp, w_up, w_down, g_final):
    assert g_mix.shape[0] == 1, "single-layer trunk"
    w = _layer_weights(g_mix[0], w_in[0], g_q_a[0], w_q_up[0], g_kv_a[0], w_kv_up[0], a_re[0], a_im[0],
                       log_step[0], b_re[0], b_im[0], c_re[0], c_im[0], d_skip[0], w_glu[0], g_attn_out[0],
                       g_ssm_out[0], w_out[0], g_mlp[0], w_up[0], w_down[0], g_final)
    tiles = dict(proj_tm=PROJ_TM, post_tm=POST_TM, ssm_tt=SSM_TT)

    bp, tp, _ = x_prompt.shape
    pos_p = jnp.arange(tp, dtype=jnp.int32)
    h0p = jnp.zeros((bp, 2 * N_STATE), F32)
    y_p, lat_p, kr_p, hr_p, hi_p = _branch(x_prompt, pos_p, None, h0p, w, **tiles)

    bs, ts, _ = x_sample.shape
    plen = cache_kv_latent.shape[2]
    pos_s = plen + jnp.arange(ts, dtype=jnp.int32)
    h0s = _pack_state(state_ssm_re[0], state_ssm_im[0])
    y_s, lat_s, kr_s, hr_s, hi_s = _branch(x_sample, pos_s, (cache_kv_latent[0], cache_k_rope[0]), h0s, w, **tiles)
    return (y_p, y_s, lat_p, kr_p, hr_p, hi_p, lat_s, kr_s, hr_s, hi_s)
```

```python
import functools
import math

import jax
import jax.numpy as jnp
from jax import lax
from jax.experimental import pallas as pl
from jax.experimental.pallas import tpu as pltpu

F32 = jnp.float32
BF16 = jnp.bfloat16

D_MODEL = 1024
N_HEADS = 8
QK_NOPE = 64
QK_ROPE = 32
ROPE_HALF = QK_ROPE // 2
V_DIM = 64
KV_LORA = 256
Q_LORA = 768
D_ATTN = N_HEADS * V_DIM
D_SSM = 512
SSM_GROUP = 16
N_GROUPS = D_SSM // SSM_GROUP
SSM_STATE = 64
N_STATE = N_GROUPS * SSM_STATE
D_FF = 4 * D_MODEL
CHUNK = 64
ROPE_THETA = 10000.0
SOFTMAX_SCALE = (QK_NOPE + QK_ROPE) ** -0.5
LOG2_E = math.log2(math.e)
EPS = 1e-6
NEG_INF = -1e30

LANES = 128
SUBLANES = 8
HEAD_PAD = LANES
D_HEADS_PAD = N_HEADS * HEAD_PAD
COL_CQ = 0
COL_CKV = Q_LORA
COL_U = Q_LORA + KV_LORA
COL_KPE = COL_U + D_SSM
COL_KPE_ROT = COL_KPE + LANES
D_IN_EXT = COL_KPE_ROT + LANES

VMEM_LIMIT = 56 * 1024 * 1024

PROJ_TM = 512
ATTN_TQ = 512
ATTN_TK_SAMPLE = 512
SSM_TT = 64
POST_TM = 512
FF_CHUNK = 1024


def _const_spec(shape):
    nd = len(shape)
    return pl.BlockSpec(shape, lambda *_: (0,) * nd, pipeline_mode=pl.Buffered(1))


def _rms(x, g):
    return x * lax.rsqrt(jnp.mean(x * x, axis=-1, keepdims=True) + EPS) * g


def _prep_kernel(are_ref, aim_ref, ls_ref, bre_ref, bim_ref, lre_ref, lim_ref, bbre_ref, bbim_ref):
    dt = jnp.exp(ls_ref[...])
    lr, li = are_ref[...], aim_ref[...]
    mag = jnp.exp(lr * dt)
    lb_re, lb_im = mag * jnp.cos(li * dt), mag * jnp.sin(li * dt)
    nr, ni = lb_re - 1.0, lb_im
    den = lr * lr + li * li
    coef_re = (nr * lr + ni * li) / den
    coef_im = (ni * lr - nr * li) / den
    lre_ref[...] = lb_re
    lim_ref[...] = lb_im
    bre, bim = bre_ref[...], bim_ref[...]
    cr, ci = coef_re[:, None, :], coef_im[:, None, :]
    bbre_ref[...] = cr * bre - ci * bim
    bbim_ref[...] = cr * bim + ci * bre


def _ssm_prep(a_re, a_im, log_step, b_re, b_im):
    g, n = a_re.shape
    p = b_re.shape[-1]
    bre_t = jnp.swapaxes(b_re, 1, 2)
    bim_t = jnp.swapaxes(b_im, 1, 2)
    return pl.pallas_call(
        _prep_kernel,
        out_shape=(jax.ShapeDtypeStruct((g, n), F32), jax.ShapeDtypeStruct((g, n), F32),
                   jax.ShapeDtypeStruct((g, p, n), F32), jax.ShapeDtypeStruct((g, p, n), F32)),
        name="ssm_prep",
    )(a_re, a_im, log_step.reshape(g, 1), bre_t, bim_t)


def _proj_kernel(x_ref, cq_ref, sq_ref, ck_ref, sk_ref, gmix_ref, win_ref, gq_ref, wq_ref, gkv_ref,
                 wk_ref, wvt_ref, e_ref,
                 q_out, lat_out, kr_out, u_out, *kv_out, v_key_tile):
    xn = _rms(x_ref[...], gmix_ref[...]).astype(BF16)
    proj = jnp.dot(xn, win_ref[...], preferred_element_type=F32)
    u_out[...] = proj[:, COL_U:COL_U + D_SSM]

    cqn = _rms(proj[:, COL_CQ:COL_CQ + Q_LORA], gq_ref[...]).astype(BF16)
    qq = jnp.dot(cqn, wq_ref[...], preferred_element_type=F32)
    cq_t, sq_t = cq_ref[...], sq_ref[...]
    for h in range(N_HEADS):
        lo = h * HEAD_PAD
        q_h = qq[:, lo:lo + HEAD_PAD] * cq_t + qq[:, D_HEADS_PAD + lo:D_HEADS_PAD + lo + HEAD_PAD] * sq_t
        q_out[:, lo:lo + HEAD_PAD] = q_h.astype(BF16)

    lat = _rms(proj[:, COL_CKV:COL_CKV + KV_LORA], gkv_ref[...])
    lat_out[...] = lat
    kr = (proj[:, COL_KPE:COL_KPE + LANES] * ck_ref[...]
          + proj[:, COL_KPE_ROT:COL_KPE_ROT + LANES] * sk_ref[...])
    kr_out[...] = kr[:, :QK_ROPE]
    if v_key_tile is None:
        return
    kpad_out, vt_out = kv_out
    latb = lat.astype(BF16)
    kpad = (jnp.dot(latb, wk_ref[...], preferred_element_type=F32)
            + jnp.dot(kr.astype(BF16), e_ref[...], preferred_element_type=F32))
    kpad_out[...] = kpad.astype(BF16)
    vt = lax.dot_general(wvt_ref[...], latb, (((1,), (1,)), ((), ())),
                         preferred_element_type=F32)
    ones = lax.broadcasted_iota(jnp.int32, vt.shape, 0) % HEAD_PAD >= V_DIM
    vt = jnp.where(ones, 1.0, vt).astype(BF16)
    for s in range(vt.shape[1] // v_key_tile):
        vt_out[s] = vt[:, s * v_key_tile:(s + 1) * v_key_tile]


def _proj(x2d, tabs, w, tm, stream_len, v_key_tile):
    n = x2d.shape[0]
    cosq, sinq, cosk, sink = tabs
    n_tab = cosq.shape[0] // tm
    row = lambda i: (i, 0)
    tab = lambda i: (i % n_tab, 0)
    out_shape = [
        jax.ShapeDtypeStruct((n, D_HEADS_PAD), BF16),
        jax.ShapeDtypeStruct((n, KV_LORA), F32),
        jax.ShapeDtypeStruct((n, QK_ROPE), F32),
        jax.ShapeDtypeStruct((n, D_SSM), F32),
    ]
    out_specs = [pl.BlockSpec((tm, D_HEADS_PAD), row), pl.BlockSpec((tm, KV_LORA), row),
                 pl.BlockSpec((tm, QK_ROPE), row), pl.BlockSpec((tm, D_SSM), row)]
    if v_key_tile is not None:
        assert stream_len % tm == 0 and tm % v_key_tile == 0
        tiles_per_stream = stream_len // tm
        out_shape += [jax.ShapeDtypeStruct((n, D_HEADS_PAD), BF16),
                      jax.ShapeDtypeStruct((n // stream_len, stream_len // v_key_tile, D_HEADS_PAD, v_key_tile),
                                           BF16)]
        out_specs += [pl.BlockSpec((tm, D_HEADS_PAD), row),
                      pl.BlockSpec((None, tm // v_key_tile, D_HEADS_PAD, v_key_tile),
                                   lambda i: (i // tiles_per_stream, i % tiles_per_stream, 0, 0))]
    return pl.pallas_call(
        functools.partial(_proj_kernel, v_key_tile=v_key_tile),
        grid=(n // tm,),
        in_specs=[
            pl.BlockSpec((tm, D_MODEL), row),
            pl.BlockSpec((tm, LANES), tab), pl.BlockSpec((tm, LANES), tab),
            pl.BlockSpec((tm, LANES), tab), pl.BlockSpec((tm, LANES), tab),
            _const_spec((1, D_MODEL)), _const_spec((D_MODEL, D_IN_EXT)),
            _const_spec((1, Q_LORA)), _const_spec((Q_LORA, 2 * D_HEADS_PAD)),
            _const_spec((1, KV_LORA)), _const_spec((KV_LORA, D_HEADS_PAD)),
            _const_spec((D_HEADS_PAD, KV_LORA)), _const_spec((LANES, D_HEADS_PAD)),
        ],
        out_specs=tuple(out_specs),
        out_shape=tuple(out_shape),
        compiler_params=pltpu.CompilerParams(dimension_semantics=("parallel",),
                                             vmem_limit_bytes=VMEM_LIMIT),
        name="proj",
    )(x2d, cosq, sinq, cosk, sink, w["g_mix"], w["w_in_ext"], w["g_q_a"], w["w_q2"], w["g_kv_a"],
      w["w_k_pad"], w["w_v_t"], w["e_rope"])


def _attn_prompt_kernel(q_ref, k_ref, vt_ref, o_ref, *, tq):
    i = pl.program_id(2)
    key_chunk = lax.broadcasted_iota(jnp.int32, (tq, tq), 0) // CHUNK
    query_chunk = lax.broadcasted_iota(jnp.int32, (tq, tq), 1) // CHUNK
    diag_mask = key_chunk <= query_chunk
    heads = [slice(h * HEAD_PAD, (h + 1) * HEAD_PAD) for h in range(2)]

    def tile(j, state, mask):
        off = pl.multiple_of(j * tq, tq)
        sts = [lax.dot_general(k_ref[pl.ds(off, tq), sl], q_ref[:, sl], (((1,), (1,)), ((), ())),
                               preferred_element_type=F32) for sl in heads]
        if mask is not None:
            sts = [jnp.where(mask, st, NEG_INF) for st in sts]
        m_new = [jnp.maximum(m, jnp.max(st, axis=0, keepdims=True)) for st, (m, _) in zip(sts, state)]
        alphas = [jnp.exp2(m - mn) for mn, (m, _) in zip(m_new, state)]
        ps = [jnp.exp2(st - mn).astype(BF16) for st, mn in zip(sts, m_new)]
        accs = [acc * a + jnp.dot(vt_ref[j, sl, :], p, preferred_element_type=F32)
                for sl, p, a, (_, acc) in zip(heads, ps, alphas, state)]
        return tuple(zip(m_new, accs))

    init = tuple((jnp.full((1, tq), NEG_INF, F32), jnp.zeros((HEAD_PAD, tq), F32)) for _ in heads)
    state = lax.fori_loop(0, i, lambda j, c: tile(j, c, None), init)
    state = tile(i, state, diag_mask)
    o_t = jnp.concatenate([acc[:V_DIM] / acc[V_DIM:V_DIM + 1] for _, acc in state], axis=0)
    o_ref[...] = o_t.T


def _attn_prompt(q, kpad, vt, tq):
    b, t, _ = q.shape
    assert tq % CHUNK == 0 and t % tq == 0 and vt.shape == (b, t // tq, D_HEADS_PAD, tq)
    return pl.pallas_call(
        functools.partial(_attn_prompt_kernel, tq=tq),
        grid=(b, N_HEADS // 2, t // tq),
        in_specs=[pl.BlockSpec((None, tq, 2 * HEAD_PAD), lambda bi, hp, i: (bi, i, hp)),
                  pl.BlockSpec((None, t, 2 * HEAD_PAD), lambda bi, hp, i: (bi, 0, hp)),
                  pl.BlockSpec((None, t // tq, 2 * HEAD_PAD, tq), lambda bi, hp, i: (bi, 0, hp, 0))],
        out_specs=pl.BlockSpec((None, tq, 2 * V_DIM), lambda bi, hp, i: (bi, i, hp)),
        out_shape=jax.ShapeDtypeStruct((b, t, D_ATTN), F32),
        compiler_params=pltpu.CompilerParams(
            dimension_semantics=("parallel", "parallel", "arbitrary"), vmem_limit_bytes=VMEM_LIMIT),
        name="attn_prompt",
    )(q, kpad, vt)


D_KCAT = KV_LORA + LANES


def _attn_sample_kernel(q_ref, plat_ref, pkr_ref, nlat_ref, nkr_ref, wabs_ref, wv_ref, o_ref, *, tk):
    tq = q_ref.shape[0]
    n_past = plat_ref.shape[0] // tk
    qcat = jnp.concatenate(
        [jnp.dot(q_ref[:, h * HEAD_PAD:(h + 1) * HEAD_PAD], wabs_ref[h], preferred_element_type=F32)
         for h in range(N_HEADS)], axis=0).astype(BF16)

    def step(lat, kr, carry):
        m, l, acc = carry
        latb = lat.astype(BF16)
        kcat = jnp.concatenate([latb, kr], axis=1)
        s = lax.dot_general(qcat, kcat, (((1,), (1,)), ((), ())), preferred_element_type=F32)
        m_new = jnp.maximum(m, jnp.max(s, axis=-1, keepdims=True))
        alpha = jnp.exp2(m - m_new)
        p = jnp.exp2(s - m_new)
        l = l * alpha + jnp.sum(p, axis=-1, keepdims=True)
        acc = acc * alpha + jnp.dot(p.astype(BF16), latb, preferred_element_type=F32)
        return m_new, l, acc

    def body(j, carry):
        off = pl.multiple_of(j * tk, tk)
        return step(plat_ref[pl.ds(off, tk), :], pkr_ref[pl.ds(off, tk), :], carry)

    rows = N_HEADS * tq
    init = (jnp.full((rows, 1), NEG_INF, F32), jnp.zeros((rows, 1), F32), jnp.zeros((rows, KV_LORA), F32))
    carry = lax.fori_loop(0, n_past, body, init)
    _, l, acc = step(nlat_ref[...], nkr_ref[...], carry)
    o_lat = (acc / l).astype(BF16)
    out = jnp.zeros((tq, D_ATTN), F32)
    for h in range(N_HEADS):
        out = out + jnp.dot(o_lat[h * tq:(h + 1) * tq], wv_ref[h], preferred_element_type=F32)
    o_ref[...] = out


def _attn_sample(q, past_lat, past_kr, new_lat, new_kr, w, tk):
    b, tq, _ = q.shape
    past = past_lat.shape[1]
    assert past % tk == 0
    blk = lambda bi: (bi, 0, 0)
    return pl.pallas_call(
        functools.partial(_attn_sample_kernel, tk=tk),
        grid=(b,),
        in_specs=[pl.BlockSpec((None, tq, D_HEADS_PAD), blk),
                  pl.BlockSpec((None, past, KV_LORA), blk), pl.BlockSpec((None, past, LANES), blk),
                  pl.BlockSpec((None, tq, KV_LORA), blk), pl.BlockSpec((None, tq, LANES), blk),
                  _const_spec((N_HEADS, HEAD_PAD, D_KCAT)), _const_spec((N_HEADS, KV_LORA, D_ATTN))],
        out_specs=pl.BlockSpec((None, tq, D_ATTN), blk),
        out_shape=jax.ShapeDtypeStruct((b, tq, D_ATTN), F32),
        compiler_params=pltpu.CompilerParams(dimension_semantics=("parallel",),
                                             vmem_limit_bytes=VMEM_LIMIT),
        name="attn_sample",
    )(q, past_lat, past_kr, new_lat, new_kr, w["w_q_abs"], w["w_v_heads"])


SSM_CH = LANES
SSM_ST = SSM_CH // SSM_GROUP * SSM_STATE
N_SSM_CHUNKS = D_SSM // SSM_CH


def _ssm_kernel(u_ref, h0_ref, lre_ref, lim_ref, bc_ref, cc_ref, d_ref, wglu_ref, gssm_ref,
                y_ref, hout_ref, x_s, h_s, hc_s, *, tt):
    ti = pl.program_id(1)

    @pl.when(ti == 0)
    def _():
        hc_s[...] = h0_ref[...]

    rows = tt * SUBLANES
    u = u_ref[...].reshape(rows, D_SSM)
    ub = u.astype(BF16)
    chunks = []
    for c in range(N_SSM_CHUNKS):
        re = slice(c * SSM_ST, (c + 1) * SSM_ST)
        im = slice(N_STATE + c * SSM_ST, N_STATE + (c + 1) * SSM_ST)
        chunks.append((re, im))
        xc = jnp.dot(ub[:, c * SSM_CH:(c + 1) * SSM_CH], bc_ref[c], preferred_element_type=F32)
        x_s[:, :, re] = xc[:, :SSM_ST].reshape(tt, SUBLANES, SSM_ST)
        x_s[:, :, im] = xc[:, SSM_ST:].reshape(tt, SUBLANES, SSM_ST)

    for re, im in chunks:
        lr = jnp.broadcast_to(lre_ref[:, re], (SUBLANES, SSM_ST))
        li = jnp.broadcast_to(lim_ref[:, re], (SUBLANES, SSM_ST))

        def step(t, carry, re=re, im=im, lr=lr, li=li):
            hr, hi = carry
            nhr = lr * hr - li * hi + x_s[t, :, re]
            nhi = lr * hi + li * hr + x_s[t, :, im]
            h_s[t, :, re] = nhr
            h_s[t, :, im] = nhi
            return nhr, nhi

        hr, hi = lax.fori_loop(0, tt, step, (hc_s[:, re], hc_s[:, im]), unroll=8)
        hc_s[:, re] = hr
        hc_s[:, im] = hi

    ys = []
    for c, (re, im) in enumerate(chunks):
        hcat = jnp.concatenate([h_s[:, :, re], h_s[:, :, im]], axis=-1).reshape(rows, 2 * SSM_ST)
        ys.append(jnp.dot(hcat.astype(BF16), cc_ref[c], preferred_element_type=F32))
    y = jnp.concatenate(ys, axis=1) + d_ref[...] * u
    y = jax.nn.gelu(y, approximate=True)
    z = jnp.dot(y.astype(BF16), wglu_ref[...], preferred_element_type=F32)
    y = y * (1.0 / (1.0 + jnp.exp(-z)))
    y_ref[...] = _rms(y, gssm_ref[...]).reshape(tt, SUBLANES, D_SSM)

    @pl.when(ti == pl.num_programs(1) - 1)
    def _():
        hout_ref[...] = hc_s[...]


def _ssm(u_tb, h0, w, tt):
    t, nb, _ = u_tb.shape
    assert nb % SUBLANES == 0 and t % tt == 0
    return pl.pallas_call(
        functools.partial(_ssm_kernel, tt=tt),
        grid=(nb // SUBLANES, t // tt),
        in_specs=[pl.BlockSpec((tt, SUBLANES, D_SSM), lambda bg, ti: (ti, bg, 0)),
                  pl.BlockSpec((SUBLANES, 2 * N_STATE), lambda bg, ti: (bg, 0)),
                  _const_spec((1, N_STATE)), _const_spec((1, N_STATE)),
                  _const_spec((N_SSM_CHUNKS, SSM_CH, 2 * SSM_ST)), _const_spec((N_SSM_CHUNKS, 2 * SSM_ST, SSM_CH)),
                  _const_spec((1, D_SSM)), _const_spec((D_SSM, D_SSM)), _const_spec((1, D_SSM))],
        out_specs=(pl.BlockSpec((tt, SUBLANES, D_SSM), lambda bg, ti: (ti, bg, 0)),
                   pl.BlockSpec((SUBLANES, 2 * N_STATE), lambda bg, ti: (bg, 0))),
        out_shape=(jax.ShapeDtypeStruct((t, nb, D_SSM), F32),
                   jax.ShapeDtypeStruct((nb, 2 * N_STATE), F32)),
        scratch_shapes=[pltpu.VMEM((tt, SUBLANES, 2 * N_STATE), F32),
                        pltpu.VMEM((tt, SUBLANES, 2 * N_STATE), F32),
                        pltpu.VMEM((SUBLANES, 2 * N_STATE), F32)],
        compiler_params=pltpu.CompilerParams(dimension_semantics=("parallel", "arbitrary"),
                                             vmem_limit_bytes=VMEM_LIMIT),
        name="ssm",
    )(u_tb, h0, w["lam_re"], w["lam_im"], w["b_chunks"], w["c_chunks"], w["d_skip"], w["w_glu"],
      w["g_ssm_out"])


def _post_kernel(x_ref, attn_ref, ssm_ref, gattn_ref, wouta_ref, wouts_ref, gmlp_ref, wup_ref, wdown_ref,
                 gfin_ref, y_ref):
    an = _rms(attn_ref[...], gattn_ref[...]).astype(BF16)
    mixed = (jnp.dot(an, wouta_ref[...], preferred_element_type=F32)
             + jnp.dot(ssm_ref[...].astype(BF16), wouts_ref[...], preferred_element_type=F32))
    h = x_ref[...] + mixed
    hn = _rms(h, gmlp_ref[...]).astype(BF16)
    acc = jnp.zeros(h.shape, F32)
    for c in range(D_FF // FF_CHUNK):
        ff = slice(c * FF_CHUNK, (c + 1) * FF_CHUNK)
        a = jnp.dot(hn, wup_ref[:, ff], preferred_element_type=F32)
        a = jnp.square(jnp.maximum(a, 0.0))
        acc = acc + jnp.dot(a.astype(BF16), wdown_ref[ff, :], preferred_element_type=F32)
    y_ref[...] = _rms(h + acc, gfin_ref[...])


def _post(x2d, attn2d, ssm2d, w, tm):
    n = x2d.shape[0]
    row = lambda i: (i, 0)
    return pl.pallas_call(
        _post_kernel,
        grid=(n // tm,),
        in_specs=[pl.BlockSpec((tm, D_MODEL), row), pl.BlockSpec((tm, D_ATTN), row),
                  pl.BlockSpec((tm, D_SSM), row),
                  _const_spec((1, D_ATTN)), _const_spec((D_ATTN, D_MODEL)), _const_spec((D_SSM, D_MODEL)),
                  _const_spec((1, D_MODEL)), _const_spec((D_MODEL, D_FF)), _const_spec((D_FF, D_MODEL)),
                  _const_spec((1, D_MODEL))],
        out_specs=pl.BlockSpec((tm, D_MODEL), row),
        out_shape=jax.ShapeDtypeStruct((n, D_MODEL), F32),
        compiler_params=pltpu.CompilerParams(dimension_semantics=("parallel",),
                                             vmem_limit_bytes=VMEM_LIMIT),
        name="post",
    )(x2d, attn2d, ssm2d, w["g_attn_out"], w["w_out_attn"], w["w_out_ssm"], w["g_mlp"], w["w_up"],
      w["w_down"], w["g_final"])


def _rope_tables(pos):
    t = pos.shape[0]
    inv_freq = ROPE_THETA ** (-(jnp.arange(ROPE_HALF, dtype=F32) * 2.0) / QK_ROPE)
    ang = pos.astype(F32)[:, None] * inv_freq[None, :]
    cc = jnp.tile(jnp.cos(ang), (1, 2))
    ss = jnp.tile(jnp.sin(ang), (1, 2))
    pad_hi = jnp.zeros((t, HEAD_PAD - QK_NOPE - QK_ROPE), F32)
    qs = SOFTMAX_SCALE * LOG2_E
    cosq = jnp.concatenate([jnp.full((t, QK_NOPE), qs, F32), cc * qs, pad_hi], axis=1)
    sinq = jnp.concatenate([jnp.zeros((t, QK_NOPE), F32), ss * qs, pad_hi], axis=1)
    pad_k = jnp.zeros((t, LANES - QK_ROPE), F32)
    cosk = jnp.concatenate([cc, pad_k], axis=1)
    sink = jnp.concatenate([ss, pad_k], axis=1)
    return cosq, sinq, cosk, sink


def _rot_cols(w_x1, w_x2):
    return -w_x2, w_x1


def _layer_weights(g_mix, w_in, g_q_a, w_q_up, g_kv_a, w_kv_up, a_re, a_im, log_step, b_re, b_im,
                   c_re, c_im, d_skip, w_glu, g_attn_out, g_ssm_out, w_out, g_mlp, w_up, w_down, g_final):
    w = {}
    w["g_mix"] = g_mix[None, :]
    w_cq, w_ckv = w_in[:, :Q_LORA], w_in[:, Q_LORA:Q_LORA + KV_LORA]
    w_kpe = w_in[:, Q_LORA + KV_LORA:Q_LORA + KV_LORA + QK_ROPE]
    w_u = w_in[:, Q_LORA + KV_LORA + QK_ROPE:]
    rot1, rot2 = _rot_cols(w_kpe[:, :ROPE_HALF], w_kpe[:, ROPE_HALF:])
    zk = jnp.zeros((D_MODEL, LANES - QK_ROPE), F32)
    w["w_in_ext"] = jnp.concatenate([w_cq, w_ckv, w_u, w_kpe, zk, rot1, rot2, zk], axis=1).astype(BF16)

    w["g_q_a"] = g_q_a[None, :]
    wq = w_q_up.reshape(Q_LORA, N_HEADS, QK_NOPE + QK_ROPE)
    nope, r1, r2 = wq[:, :, :QK_NOPE], wq[:, :, QK_NOPE:QK_NOPE + ROPE_HALF], wq[:, :, QK_NOPE + ROPE_HALF:]
    zq = jnp.zeros((Q_LORA, N_HEADS, HEAD_PAD - QK_NOPE - QK_ROPE), F32)
    rot1, rot2 = _rot_cols(r1, r2)
    wq_pad = jnp.concatenate([nope, r1, r2, zq], axis=2).reshape(Q_LORA, D_HEADS_PAD)
    wq_rot = jnp.concatenate([jnp.zeros_like(nope), rot1, rot2, zq], axis=2).reshape(Q_LORA, D_HEADS_PAD)
    w["w_q2"] = jnp.concatenate([wq_pad, wq_rot], axis=1).astype(BF16)

    w["g_kv_a"] = g_kv_a[None, :]
    wkv = w_kv_up.reshape(KV_LORA, N_HEADS, QK_NOPE + V_DIM)
    zkv = jnp.zeros((KV_LORA, N_HEADS, HEAD_PAD - QK_NOPE), F32)
    wk_pad = jnp.concatenate([wkv[:, :, :QK_NOPE], zkv], axis=2).reshape(KV_LORA, D_HEADS_PAD)
    wv_pad = jnp.concatenate([wkv[:, :, QK_NOPE:], zkv], axis=2).reshape(KV_LORA, D_HEADS_PAD)
    w["w_k_pad"] = wk_pad.astype(BF16)
    w["w_v_t"] = wv_pad.T.astype(BF16)
    wk_t = jnp.transpose(wkv[:, :, :QK_NOPE], (1, 2, 0))
    q_abs = jnp.zeros((N_HEADS, HEAD_PAD, D_KCAT), F32).at[:, :QK_NOPE, :KV_LORA].set(wk_t)
    q_abs = q_abs.at[:, QK_NOPE + jnp.arange(QK_ROPE), KV_LORA + jnp.arange(QK_ROPE)].set(1.0)
    w["w_q_abs"] = q_abs.astype(BF16)
    wv_h = jnp.transpose(wkv[:, :, QK_NOPE:], (1, 0, 2))
    w["w_v_heads"] = jnp.einsum("hce,hg->hcge", wv_h, jnp.eye(N_HEADS, dtype=F32)).reshape(
        N_HEADS, KV_LORA, D_ATTN).astype(BF16)
    place = jnp.zeros((LANES, HEAD_PAD), F32).at[jnp.arange(QK_ROPE), QK_NOPE + jnp.arange(QK_ROPE)].set(1.0)
    w["e_rope"] = jnp.tile(place, (1, N_HEADS)).astype(BF16)

    lam_re, lam_im, bb_re, bb_im = _ssm_prep(a_re, a_im, log_step, b_re, b_im)
    w["lam_re"] = lam_re.reshape(1, N_STATE)
    w["lam_im"] = lam_im.reshape(1, N_STATE)
    gpc = SSM_CH // SSM_GROUP
    eye = jnp.eye(gpc, dtype=F32)
    by_chunk = lambda m: m.reshape(N_SSM_CHUNKS, gpc, SSM_GROUP, SSM_STATE)
    blk = lambda m: jnp.einsum("cgpn,gh->cgphn", by_chunk(m), eye).reshape(N_SSM_CHUNKS, SSM_CH, SSM_ST)
    w["b_chunks"] = jnp.concatenate([blk(bb_re), blk(bb_im)], axis=2).astype(BF16)
    blk_t = lambda m: jnp.einsum("cgpn,gh->cgnhp", by_chunk(m), eye).reshape(N_SSM_CHUNKS, SSM_ST, SSM_CH)
    w["c_chunks"] = jnp.concatenate([blk_t(c_re), blk_t(-c_im)], axis=1).astype(BF16)
    w["d_skip"] = d_skip[None, :]
    w["w_glu"] = w_glu.astype(BF16)
    w["g_ssm_out"] = g_ssm_out[None, :]

    w["g_attn_out"] = g_attn_out[None, :]
    w["w_out_attn"] = w_out[:D_ATTN].astype(BF16)
    w["w_out_ssm"] = w_out[D_ATTN:].astype(BF16)
    w["g_mlp"] = g_mlp[None, :]
    w["w_up"] = w_up.astype(BF16)
    w["w_down"] = w_down.astype(BF16)
    w["g_final"] = g_final[None, :]
    return w


def _pack_state(h_re, h_im):
    nb = h_re.shape[0]
    return jnp.concatenate([h_re.reshape(nb, N_STATE), h_im.reshape(nb, N_STATE)], axis=1)


def _unpack_state(h):
    nb = h.shape[0]
    return (h[:, :N_STATE].reshape(nb, N_GROUPS, SSM_STATE), h[:, N_STATE:].reshape(nb, N_GROUPS, SSM_STATE))


def _branch(x, pos, past, h0, w, *, proj_tm, post_tm, ssm_tt):
    b, t, _ = x.shape
    n = b * t
    x2d = x.reshape(n, D_MODEL)
    tabs = _rope_tables(pos)
    if t < proj_tm:
        assert proj_tm % t == 0
        tabs = tuple(jnp.tile(a, (proj_tm // t, 1)) for a in tabs)
    else:
        assert t % proj_tm == 0
    if past is None:
        tq = min(ATTN_TQ, t)
        q, lat, kr, u, kpad, vt = _proj(x2d, tabs, w, proj_tm, t, tq)
        attn = _attn_prompt(q.reshape(b, t, D_HEADS_PAD), kpad.reshape(b, t, D_HEADS_PAD), vt, tq)
    else:
        q, lat, kr, u = _proj(x2d, tabs, w, proj_tm, t, None)
        past_lat, past_kr = past
        plen = past_lat.shape[1]
        assert plen % CHUNK == 0 and t <= CHUNK
        lane_pad = lambda a: jnp.pad(a, ((0, 0), (0, 0), (0, LANES - QK_ROPE))).astype(BF16)
        attn = _attn_sample(q.reshape(b, t, D_HEADS_PAD), past_lat, lane_pad(past_kr),
                            lat.reshape(b, t, KV_LORA), lane_pad(kr.reshape(b, t, QK_ROPE)), w,
                            min(ATTN_TK_SAMPLE, plen))
    u_tb = jnp.swapaxes(u.reshape(b, t, D_SSM), 0, 1)
    y_tb, h_fin = _ssm(u_tb, h0, w, min(ssm_tt, t))
    ssm2d = jnp.swapaxes(y_tb, 0, 1).reshape(n, D_SSM)
    y = _post(x2d, attn.reshape(n, D_ATTN), ssm2d, w, post_tm)
    h_re, h_im = _unpack_state(h_fin)
    return (y.reshape(b, t, D_MODEL), lat.reshape(1, b, t, KV_LORA), kr.reshape(1, b, t, QK_ROPE),
            h_re[None], h_im[None])


def kernel(x_prompt, x_sample, cache_kv_latent, cache_k_rope, state_ssm_re, state_ssm_im, g_mix, w_in, g_q_a,
           w_q_up, g_kv_a, w_kv_up, a_re, a_im, log_step, b_re, b_im, c_re, c_im, d_skip, w_glu, g_attn_out,
           g_ssm_out, w_out, g_mlp, w_up, w_down, g_final):
    assert g_mix.shape[0] == 1, "single-layer trunk"
    w = _layer_weights(g_mix[0], w_in[0], g_q_a[0], w_q_up[0], g_kv_a[0], w_kv_up[0], a_re[0], a_im[0],
                       log_step[0], b_re[0], b_im[0], c_re[0], c_im[0], d_skip[0], w_glu[0], g_attn_out[0],
                       g_ssm_out[0], w_out[0], g_mlp[0], w_up[0], w_down[0], g_final)
    tiles = dict(proj_tm=PROJ_TM, post_tm=POST_TM, ssm_tt=SSM_TT)

    bp, tp, _ = x_prompt.shape
    pos_p = jnp.arange(tp, dtype=jnp.int32)
    h0p = jnp.zeros((bp, 2 * N_STATE), F32)
    y_p, lat_p, kr_p, hr_p, hi_p = _branch(x_prompt, pos_p, None, h0p, w, **tiles)

    bs, ts, _ = x_sample.shape
    plen = cache_kv_latent.shape[2]
    pos_s = plen + jnp.arange(ts, dtype=jnp.int32)
    h0s = _pack_state(state_ssm_re[0], state_ssm_im[0])
    y_s, lat_s, kr_s, hr_s, hi_s = _branch(x_sample, pos_s, (cache_kv_latent[0], cache_k_rope[0]), h0s, w, **tiles)
    return (y_p, y_s, lat_p, kr_p, hr_p, hi_p, lat_s, kr_s, hr_s, hi_s)
```

```python
import functools
import math

import jax
import jax.numpy as jnp
from jax import lax
from jax.experimental import pallas as pl
from jax.experimental.pallas import tpu as pltpu

F32 = jnp.float32
BF16 = jnp.bfloat16

D_MODEL = 1024
N_HEADS = 8
QK_NOPE = 64
QK_ROPE = 32
ROPE_HALF = QK_ROPE // 2
V_DIM = 64
KV_LORA = 256
Q_LORA = 768
D_ATTN = N_HEADS * V_DIM
D_SSM = 512
SSM_GROUP = 16
N_GROUPS = D_SSM // SSM_GROUP
SSM_STATE = 64
N_STATE = N_GROUPS * SSM_STATE
D_FF = 4 * D_MODEL
CHUNK = 64
ROPE_THETA = 10000.0
SOFTMAX_SCALE = (QK_NOPE + QK_ROPE) ** -0.5
LOG2_E = math.log2(math.e)
EPS = 1e-6
NEG_INF = -1e30

LANES = 128
SUBLANES = 8
HEAD_PAD = LANES
D_HEADS_PAD = N_HEADS * HEAD_PAD
COL_CQ = 0
COL_CKV = Q_LORA
COL_U = Q_LORA + KV_LORA
COL_KPE = COL_U + D_SSM
COL_KPE_ROT = COL_KPE + LANES
D_IN_EXT = COL_KPE_ROT + LANES

VMEM_LIMIT = 56 * 1024 * 1024

PROJ_TM = 512
ATTN_TQ = 512
ATTN_TK_SAMPLE = 512
ATTN_SUB = 2
SSM_TT = 64
POST_TM = 512
FF_CHUNK = 1024


def _const_spec(shape):
    nd = len(shape)
    return pl.BlockSpec(shape, lambda *_: (0,) * nd, pipeline_mode=pl.Buffered(1))


def _rms(x, g):
    return x * lax.rsqrt(jnp.mean(x * x, axis=-1, keepdims=True) + EPS) * g


def _prep_kernel(are_ref, aim_ref, ls_ref, bre_ref, bim_ref, lre_ref, lim_ref, bbre_ref, bbim_ref):
    dt = jnp.exp(ls_ref[...])
    lr, li = are_ref[...], aim_ref[...]
    mag = jnp.exp(lr * dt)
    lb_re, lb_im = mag * jnp.cos(li * dt), mag * jnp.sin(li * dt)
    nr, ni = lb_re - 1.0, lb_im
    den = lr * lr + li * li
    coef_re = (nr * lr + ni * li) / den
    coef_im = (ni * lr - nr * li) / den
    lre_ref[...] = lb_re
    lim_ref[...] = lb_im
    bre, bim = bre_ref[...], bim_ref[...]
    cr, ci = coef_re[:, None, :], coef_im[:, None, :]
    bbre_ref[...] = cr * bre - ci * bim
    bbim_ref[...] = cr * bim + ci * bre


def _ssm_prep(a_re, a_im, log_step, b_re, b_im):
    g, n = a_re.shape
    p = b_re.shape[-1]
    bre_t = jnp.swapaxes(b_re, 1, 2)
    bim_t = jnp.swapaxes(b_im, 1, 2)
    return pl.pallas_call(
        _prep_kernel,
        out_shape=(jax.ShapeDtypeStruct((g, n), F32), jax.ShapeDtypeStruct((g, n), F32),
                   jax.ShapeDtypeStruct((g, p, n), F32), jax.ShapeDtypeStruct((g, p, n), F32)),
        name="ssm_prep",
    )(a_re, a_im, log_step.reshape(g, 1), bre_t, bim_t)


def _proj_kernel(x_ref, cq_ref, sq_ref, ck_ref, sk_ref, gmix_ref, win_ref, gq_ref, wq_ref, gkv_ref,
                 wk_ref, wvt_ref, e_ref,
                 q_out, lat_out, kr_out, u_out, *kv_out, v_key_tile):
    xn = _rms(x_ref[...], gmix_ref[...]).astype(BF16)
    proj = jnp.dot(xn, win_ref[...], preferred_element_type=F32)
    u_out[...] = proj[:, COL_U:COL_U + D_SSM]

    cqn = _rms(proj[:, COL_CQ:COL_CQ + Q_LORA], gq_ref[...]).astype(BF16)
    qq = jnp.dot(cqn, wq_ref[...], preferred_element_type=F32)
    cq_t, sq_t = cq_ref[...], sq_ref[...]
    for h in range(N_HEADS):
        lo = h * HEAD_PAD
        q_h = qq[:, lo:lo + HEAD_PAD] * cq_t + qq[:, D_HEADS_PAD + lo:D_HEADS_PAD + lo + HEAD_PAD] * sq_t
        q_out[:, lo:lo + HEAD_PAD] = q_h.astype(BF16)

    lat = _rms(proj[:, COL_CKV:COL_CKV + KV_LORA], gkv_ref[...])
    lat_out[...] = lat
    kr = (proj[:, COL_KPE:COL_KPE + LANES] * ck_ref[...]
          + proj[:, COL_KPE_ROT:COL_KPE_ROT + LANES] * sk_ref[...])
    kr_out[...] = kr[:, :QK_ROPE]
    if v_key_tile is None:
        return
    kpad_out, vt_out = kv_out
    latb = lat.astype(BF16)
    kpad = (jnp.dot(latb, wk_ref[...], preferred_element_type=F32)
            + jnp.dot(kr.astype(BF16), e_ref[...], preferred_element_type=F32))
    kpad_out[...] = kpad.astype(BF16)
    vt = lax.dot_general(wvt_ref[...], latb, (((1,), (1,)), ((), ())),
                         preferred_element_type=F32)
    ones = lax.broadcasted_iota(jnp.int32, vt.shape, 0) % HEAD_PAD >= V_DIM
    vt = jnp.where(ones, 1.0, vt).astype(BF16)
    for s in range(vt.shape[1] // v_key_tile):
        vt_out[s] = vt[:, s * v_key_tile:(s + 1) * v_key_tile]


def _proj(x2d, tabs, w, tm, stream_len, v_key_tile):
    n = x2d.shape[0]
    cosq, sinq, cosk, sink = tabs
    n_tab = cosq.shape[0] // tm
    row = lambda i: (i, 0)
    tab = lambda i: (i % n_tab, 0)
    out_shape = [
        jax.ShapeDtypeStruct((n, D_HEADS_PAD), BF16),
        jax.ShapeDtypeStruct((n, KV_LORA), F32),
        jax.ShapeDtypeStruct((n, QK_ROPE), F32),
        jax.ShapeDtypeStruct((n, D_SSM), F32),
    ]
    out_specs = [pl.BlockSpec((tm, D_HEADS_PAD), row), pl.BlockSpec((tm, KV_LORA), row),
                 pl.BlockSpec((tm, QK_ROPE), row), pl.BlockSpec((tm, D_SSM), row)]
    if v_key_tile is not None:
        assert stream_len % tm == 0 and tm % v_key_tile == 0
        tiles_per_stream = stream_len // tm
        out_shape += [jax.ShapeDtypeStruct((n, D_HEADS_PAD), BF16),
                      jax.ShapeDtypeStruct((n // stream_len, stream_len // v_key_tile, D_HEADS_PAD, v_key_tile),
                                           BF16)]
        out_specs += [pl.BlockSpec((tm, D_HEADS_PAD), row),
                      pl.BlockSpec((None, tm // v_key_tile, D_HEADS_PAD, v_key_tile),
                                   lambda i: (i // tiles_per_stream, i % tiles_per_stream, 0, 0))]
    return pl.pallas_call(
        functools.partial(_proj_kernel, v_key_tile=v_key_tile),
        grid=(n // tm,),
        in_specs=[
            pl.BlockSpec((tm, D_MODEL), row),
            pl.BlockSpec((tm, LANES), tab), pl.BlockSpec((tm, LANES), tab),
            pl.BlockSpec((tm, LANES), tab), pl.BlockSpec((tm, LANES), tab),
            _const_spec((1, D_MODEL)), _const_spec((D_MODEL, D_IN_EXT)),
            _const_spec((1, Q_LORA)), _const_spec((Q_LORA, 2 * D_HEADS_PAD)),
            _const_spec((1, KV_LORA)), _const_spec((KV_LORA, D_HEADS_PAD)),
            _const_spec((D_HEADS_PAD, KV_LORA)), _const_spec((LANES, D_HEADS_PAD)),
        ],
        out_specs=tuple(out_specs),
        out_shape=tuple(out_shape),
        compiler_params=pltpu.CompilerParams(dimension_semantics=("parallel",),
                                             vmem_limit_bytes=VMEM_LIMIT),
        name="proj",
    )(x2d, cosq, sinq, cosk, sink, w["g_mix"], w["w_in_ext"], w["g_q_a"], w["w_q2"], w["g_kv_a"],
      w["w_k_pad"], w["w_v_t"], w["e_rope"])


def _attn_prompt_kernel(q_ref, k_ref, vt_ref, o_ref, s00, s01, s10, s11, *, tq):
    i = pl.program_id(2)
    key_chunk = lax.broadcasted_iota(jnp.int32, (tq, tq), 0) // CHUNK
    query_chunk = lax.broadcasted_iota(jnp.int32, (tq, tq), 1) // CHUNK
    diag_mask = key_chunk <= query_chunk
    heads = [slice(h * HEAD_PAD, (h + 1) * HEAD_PAD) for h in range(2)]
    slots = ((s00, s01), (s10, s11))
    n_sub = ATTN_SUB
    ts = tq // n_sub

    def scores(j, slot):
        off = pl.multiple_of(j * tq, tq)
        for sl, buf in zip(heads, slots[slot]):
            buf[...] = lax.dot_general(k_ref[pl.ds(off, tq), sl], q_ref[:, sl], (((1,), (1,)), ((), ())),
                                       preferred_element_type=F32)

    def consume(j, slot, state, mask):
        for s in range(n_sub):
            rows = slice(s * ts, (s + 1) * ts)
            new_state = []
            for sl, buf, (m, acc) in zip(heads, slots[slot], state):
                st = buf[rows, :]
                if mask is not None:
                    st = jnp.where(mask[rows], st, NEG_INF)
                m_new = jnp.maximum(m, jnp.max(st, axis=0, keepdims=True))
                alpha = jnp.exp2(m - m_new)
                p = jnp.exp2(st - m_new).astype(BF16)
                acc = acc * alpha + jnp.dot(vt_ref[j, sl, rows], p, preferred_element_type=F32)
                new_state.append((m_new, acc))
            state = tuple(new_state)
        return state

    def pair(p, state):
        scores(2 * p + 1, 1)
        state = consume(2 * p, 0, state, None)
        scores(2 * p + 2, 0)
        return consume(2 * p + 1, 1, state, None)

    init = tuple((jnp.full((1, tq), NEG_INF, F32), jnp.zeros((HEAD_PAD, tq), F32)) for _ in heads)
    scores(0, 0)
    state = lax.fori_loop(0, i // 2, pair, init)

    def even_tail(state):
        return consume(i, 0, state, diag_mask)

    def odd_tail(state):
        scores(i, 1)
        state = consume(i - 1, 0, state, None)
        return consume(i, 1, state, diag_mask)

    state = lax.cond(i % 2 == 0, even_tail, odd_tail, state)
    o_t = jnp.concatenate([acc[:V_DIM] / acc[V_DIM:V_DIM + 1] for _, acc in state], axis=0)
    o_ref[...] = o_t.T


def _attn_prompt(q, kpad, vt, tq):
    b, t, _ = q.shape
    assert tq % CHUNK == 0 and t % tq == 0 and vt.shape == (b, t // tq, D_HEADS_PAD, tq)
    return pl.pallas_call(
        functools.partial(_attn_prompt_kernel, tq=tq),
        grid=(b, N_HEADS // 2, t // tq),
        in_specs=[pl.BlockSpec((None, tq, 2 * HEAD_PAD), lambda bi, hp, i: (bi, i, hp)),
                  pl.BlockSpec((None, t, 2 * HEAD_PAD), lambda bi, hp, i: (bi, 0, hp)),
                  pl.BlockSpec((None, t // tq, 2 * HEAD_PAD, tq), lambda bi, hp, i: (bi, 0, hp, 0))],
        out_specs=pl.BlockSpec((None, tq, 2 * V_DIM), lambda bi, hp, i: (bi, i, hp)),
        out_shape=jax.ShapeDtypeStruct((b, t, D_ATTN), F32),
        scratch_shapes=[pltpu.VMEM((tq, tq), F32)] * 4,
        compiler_params=pltpu.CompilerParams(
            dimension_semantics=("parallel", "parallel", "arbitrary"), vmem_limit_bytes=VMEM_LIMIT),
        name="attn_prompt",
    )(q, kpad, vt)


D_KCAT = KV_LORA + LANES


def _attn_sample_kernel(q_ref, plat_ref, pkr_ref, nlat_ref, nkr_ref, wabs_ref, wv_ref, o_ref, *, tk):
    tq = q_ref.shape[0]
    n_past = plat_ref.shape[0] // tk
    qcat = jnp.concatenate(
        [jnp.dot(q_ref[:, h * HEAD_PAD:(h + 1) * HEAD_PAD], wabs_ref[h], preferred_element_type=F32)
         for h in range(N_HEADS)], axis=0).astype(BF16)

    def step(lat, kr, carry):
        m, l, acc = carry
        latb = lat.astype(BF16)
        kcat = jnp.concatenate([latb, kr], axis=1)
        s = lax.dot_general(qcat, kcat, (((1,), (1,)), ((), ())), preferred_element_type=F32)
        m_new = jnp.maximum(m, jnp.max(s, axis=-1, keepdims=True))
        alpha = jnp.exp2(m - m_new)
        p = jnp.exp2(s - m_new)
        l = l * alpha + jnp.sum(p, axis=-1, keepdims=True)
        acc = acc * alpha + jnp.dot(p.astype(BF16), latb, preferred_element_type=F32)
        return m_new, l, acc

    def body(j, carry):
        off = pl.multiple_of(j * tk, tk)
        return step(plat_ref[pl.ds(off, tk), :], pkr_ref[pl.ds(off, tk), :], carry)

    rows = N_HEADS * tq
    init = (jnp.full((rows, 1), NEG_INF, F32), jnp.zeros((rows, 1), F32), jnp.zeros((rows, KV_LORA), F32))
    carry = lax.fori_loop(0, n_past, body, init)
    _, l, acc = step(nlat_ref[...], nkr_ref[...], carry)
    o_lat = (acc / l).astype(BF16)
    out = jnp.zeros((tq, D_ATTN), F32)
    for h in range(N_HEADS):
        out = out + jnp.dot(o_lat[h * tq:(h + 1) * tq], wv_ref[h], preferred_element_type=F32)
    o_ref[...] = out


def _attn_sample(q, past_lat, past_kr, new_lat, new_kr, w, tk):
    b, tq, _ = q.shape
    past = past_lat.shape[1]
    assert past % tk == 0
    blk = lambda bi: (bi, 0, 0)
    return pl.pallas_call(
        functools.partial(_attn_sample_kernel, tk=tk),
        grid=(b,),
        in_specs=[pl.BlockSpec((None, tq, D_HEADS_PAD), blk),
                  pl.BlockSpec((None, past, KV_LORA), blk), pl.BlockSpec((None, past, LANES), blk),
                  pl.BlockSpec((None, tq, KV_LORA), blk), pl.BlockSpec((None, tq, LANES), blk),
                  _const_spec((N_HEADS, HEAD_PAD, D_KCAT)), _const_spec((N_HEADS, KV_LORA, D_ATTN))],
        out_specs=pl.BlockSpec((None, tq, D_ATTN), blk),
        out_shape=jax.ShapeDtypeStruct((b, tq, D_ATTN), F32),
        compiler_params=pltpu.CompilerParams(dimension_semantics=("parallel",),
                                             vmem_limit_bytes=VMEM_LIMIT),
        name="attn_sample",
    )(q, past_lat, past_kr, new_lat, new_kr, w["w_q_abs"], w["w_v_heads"])


SSM_CH = LANES
SSM_ST = SSM_CH // SSM_GROUP * SSM_STATE
N_SSM_CHUNKS = D_SSM // SSM_CH


def _ssm_kernel(u_ref, h0_ref, lre_ref, lim_ref, bc_ref, cc_ref, d_ref, wglu_ref, gssm_ref,
                y_ref, hout_ref, x_s, h_s, hc_s, *, tt):
    ti = pl.program_id(1)

    @pl.when(ti == 0)
    def _():
        hc_s[...] = h0_ref[...]

    rows = tt * SUBLANES
    u = u_ref[...].reshape(rows, D_SSM)
    ub = u.astype(BF16)
    chunks = []
    for c in range(N_SSM_CHUNKS):
        re = slice(c * SSM_ST, (c + 1) * SSM_ST)
        im = slice(N_STATE + c * SSM_ST, N_STATE + (c + 1) * SSM_ST)
        chunks.append((re, im))
        xc = jnp.dot(ub[:, c * SSM_CH:(c + 1) * SSM_CH], bc_ref[c], preferred_element_type=F32)
        x_s[:, :, re] = xc[:, :SSM_ST].reshape(tt, SUBLANES, SSM_ST)
        x_s[:, :, im] = xc[:, SSM_ST:].reshape(tt, SUBLANES, SSM_ST)

    for re, im in chunks:
        lr = jnp.broadcast_to(lre_ref[:, re], (SUBLANES, SSM_ST))
        li = jnp.broadcast_to(lim_ref[:, re], (SUBLANES, SSM_ST))

        def step(t, carry, re=re, im=im, lr=lr, li=li):
            hr, hi = carry
            nhr = lr * hr - li * hi + x_s[t, :, re]
            nhi = lr * hi + li * hr + x_s[t, :, im]
            h_s[t, :, re] = nhr
            h_s[t, :, im] = nhi
            return nhr, nhi

        hr, hi = lax.fori_loop(0, tt, step, (hc_s[:, re], hc_s[:, im]), unroll=8)
        hc_s[:, re] = hr
        hc_s[:, im] = hi

    ys = []
    for c, (re, im) in enumerate(chunks):
        hcat = jnp.concatenate([h_s[:, :, re], h_s[:, :, im]], axis=-1).reshape(rows, 2 * SSM_ST)
        ys.append(jnp.dot(hcat.astype(BF16), cc_ref[c], preferred_element_type=F32))
    y = jnp.concatenate(ys, axis=1) + d_ref[...] * u
    y = jax.nn.gelu(y, approximate=True)
    z = jnp.dot(y.astype(BF16), wglu_ref[...], preferred_element_type=F32)
    y = y * (1.0 / (1.0 + jnp.exp(-z)))
    y_ref[...] = _rms(y, gssm_ref[...]).reshape(tt, SUBLANES, D_SSM)

    @pl.when(ti == pl.num_programs(1) - 1)
    def _():
        hout_ref[...] = hc_s[...]


def _ssm(u_tb, h0, w, tt):
    t, nb, _ = u_tb.shape
    assert nb % SUBLANES == 0 and t % tt == 0
    return pl.pallas_call(
        functools.partial(_ssm_kernel, tt=tt),
        grid=(nb // SUBLANES, t // tt),
        in_specs=[pl.BlockSpec((tt, SUBLANES, D_SSM), lambda bg, ti: (ti, bg, 0)),
                  pl.BlockSpec((SUBLANES, 2 * N_STATE), lambda bg, ti: (bg, 0)),
                  _const_spec((1, N_STATE)), _const_spec((1, N_STATE)),
                  _const_spec((N_SSM_CHUNKS, SSM_CH, 2 * SSM_ST)), _const_spec((N_SSM_CHUNKS, 2 * SSM_ST, SSM_CH)),
                  _const_spec((1, D_SSM)), _const_spec((D_SSM, D_SSM)), _const_spec((1, D_SSM))],
        out_specs=(pl.BlockSpec((tt, SUBLANES, D_SSM), lambda bg, ti: (ti, bg, 0)),
                   pl.BlockSpec((SUBLANES, 2 * N_STATE), lambda bg, ti: (bg, 0))),
        out_shape=(jax.ShapeDtypeStruct((t, nb, D_SSM), F32),
                   jax.ShapeDtypeStruct((nb, 2 * N_STATE), F32)),
        scratch_shapes=[pltpu.VMEM((tt, SUBLANES, 2 * N_STATE), F32),
                        pltpu.VMEM((tt, SUBLANES, 2 * N_STATE), F32),
                        pltpu.VMEM((SUBLANES, 2 * N_STATE), F32)],
        compiler_params=pltpu.CompilerParams(dimension_semantics=("parallel", "arbitrary"),
                                             vmem_limit_bytes=VMEM_LIMIT),
        name="ssm",
    )(u_tb, h0, w["lam_re"], w["lam_im"], w["b_chunks"], w["c_chunks"], w["d_skip"], w["w_glu"],
      w["g_ssm_out"])


def _post_kernel(x_ref, attn_ref, ssm_ref, gattn_ref, wouta_ref, wouts_ref, gmlp_ref, wup_ref, wdown_ref,
                 gfin_ref, y_ref):
    an = _rms(attn_ref[...], gattn_ref[...]).astype(BF16)
    mixed = (jnp.dot(an, wouta_ref[...], preferred_element_type=F32)
             + jnp.dot(ssm_ref[...].astype(BF16), wouts_ref[...], preferred_element_type=F32))
    h = x_ref[...] + mixed
    hn = _rms(h, gmlp_ref[...]).astype(BF16)
    acc = jnp.zeros(h.shape, F32)
    for c in range(D_FF // FF_CHUNK):
        ff = slice(c * FF_CHUNK, (c + 1) * FF_CHUNK)
        a = jnp.dot(hn, wup_ref[:, ff], preferred_element_type=F32)
        a = jnp.square(jnp.maximum(a, 0.0))
        acc = acc + jnp.dot(a.astype(BF16), wdown_ref[ff, :], preferred_element_type=F32)
    y_ref[...] = _rms(h + acc, gfin_ref[...])


def _post(x2d, attn2d, ssm2d, w, tm):
    n = x2d.shape[0]
    row = lambda i: (i, 0)
    return pl.pallas_call(
        _post_kernel,
        grid=(n // tm,),
        in_specs=[pl.BlockSpec((tm, D_MODEL), row), pl.BlockSpec((tm, D_ATTN), row),
                  pl.BlockSpec((tm, D_SSM), row),
                  _const_spec((1, D_ATTN)), _const_spec((D_ATTN, D_MODEL)), _const_spec((D_SSM, D_MODEL)),
                  _const_spec((1, D_MODEL)), _const_spec((D_MODEL, D_FF)), _const_spec((D_FF, D_MODEL)),
                  _const_spec((1, D_MODEL))],
        out_specs=pl.BlockSpec((tm, D_MODEL), row),
        out_shape=jax.ShapeDtypeStruct((n, D_MODEL), F32),
        compiler_params=pltpu.CompilerParams(dimension_semantics=("parallel",),
                                             vmem_limit_bytes=VMEM_LIMIT),
        name="post",
    )(x2d, attn2d, ssm2d, w["g_attn_out"], w["w_out_attn"], w["w_out_ssm"], w["g_mlp"], w["w_up"],
      w["w_down"], w["g_final"])


def _rope_tables(pos):
    t = pos.shape[0]
    inv_freq = ROPE_THETA ** (-(jnp.arange(ROPE_HALF, dtype=F32) * 2.0) / QK_ROPE)
    ang = pos.astype(F32)[:, None] * inv_freq[None, :]
    cc = jnp.tile(jnp.cos(ang), (1, 2))
    ss = jnp.tile(jnp.sin(ang), (1, 2))
    pad_hi = jnp.zeros((t, HEAD_PAD - QK_NOPE - QK_ROPE), F32)
    qs = SOFTMAX_SCALE * LOG2_E
    cosq = jnp.concatenate([jnp.full((t, QK_NOPE), qs, F32), cc * qs, pad_hi], axis=1)
    sinq = jnp.concatenate([jnp.zeros((t, QK_NOPE), F32), ss * qs, pad_hi], axis=1)
    pad_k = jnp.zeros((t, LANES - QK_ROPE), F32)
    cosk = jnp.concatenate([cc, pad_k], axis=1)
    sink = jnp.concatenate([ss, pad_k], axis=1)
    return cosq, sinq, cosk, sink


def _rot_cols(w_x1, w_x2):
    return -w_x2, w_x1


def _layer_weights(g_mix, w_in, g_q_a, w_q_up, g_kv_a, w_kv_up, a_re, a_im, log_step, b_re, b_im,
                   c_re, c_im, d_skip, w_glu, g_attn_out, g_ssm_out, w_out, g_mlp, w_up, w_down, g_final):
    w = {}
    w["g_mix"] = g_mix[None, :]
    w_cq, w_ckv = w_in[:, :Q_LORA], w_in[:, Q_LORA:Q_LORA + KV_LORA]
    w_kpe = w_in[:, Q_LORA + KV_LORA:Q_LORA + KV_LORA + QK_ROPE]
    w_u = w_in[:, Q_LORA + KV_LORA + QK_ROPE:]
    rot1, rot2 = _rot_cols(w_kpe[:, :ROPE_HALF], w_kpe[:, ROPE_HALF:])
    zk = jnp.zeros((D_MODEL, LANES - QK_ROPE), F32)
    w["w_in_ext"] = jnp.concatenate([w_cq, w_ckv, w_u, w_kpe, zk, rot1, rot2, zk], axis=1).astype(BF16)

    w["g_q_a"] = g_q_a[None, :]
    wq = w_q_up.reshape(Q_LORA, N_HEADS, QK_NOPE + QK_ROPE)
    nope, r1, r2 = wq[:, :, :QK_NOPE], wq[:, :, QK_NOPE:QK_NOPE + ROPE_HALF], wq[:, :, QK_NOPE + ROPE_HALF:]
    zq = jnp.zeros((Q_LORA, N_HEADS, HEAD_PAD - QK_NOPE - QK_ROPE), F32)
    rot1, rot2 = _rot_cols(r1, r2)
    wq_pad = jnp.concatenate([nope, r1, r2, zq], axis=2).reshape(Q_LORA, D_HEADS_PAD)
    wq_rot = jnp.concatenate([jnp.zeros_like(nope), rot1, rot2, zq], axis=2).reshape(Q_LORA, D_HEADS_PAD)
    w["w_q2"] = jnp.concatenate([wq_pad, wq_rot], axis=1).astype(BF16)

    w["g_kv_a"] = g_kv_a[None, :]
    wkv = w_kv_up.reshape(KV_LORA, N_HEADS, QK_NOPE + V_DIM)
    zkv = jnp.zeros((KV_LORA, N_HEADS, HEAD_PAD - QK_NOPE), F32)
    wk_pad = jnp.concatenate([wkv[:, :, :QK_NOPE], zkv], axis=2).reshape(KV_LORA, D_HEADS_PAD)
    wv_pad = jnp.concatenate([wkv[:, :, QK_NOPE:], zkv], axis=2).reshape(KV_LORA, D_HEADS_PAD)
    w["w_k_pad"] = wk_pad.astype(BF16)
    w["w_v_t"] = wv_pad.T.astype(BF16)
    wk_t = jnp.transpose(wkv[:, :, :QK_NOPE], (1, 2, 0))
    q_abs = jnp.zeros((N_HEADS, HEAD_PAD, D_KCAT), F32).at[:, :QK_NOPE, :KV_LORA].set(wk_t)
    q_abs = q_abs.at[:, QK_NOPE + jnp.arange(QK_ROPE), KV_LORA + jnp.arange(QK_ROPE)].set(1.0)
    w["w_q_abs"] = q_abs.astype(BF16)
    wv_h = jnp.transpose(wkv[:, :, QK_NOPE:], (1, 0, 2))
    w["w_v_heads"] = jnp.einsum("hce,hg->hcge", wv_h, jnp.eye(N_HEADS, dtype=F32)).reshape(
        N_HEADS, KV_LORA, D_ATTN).astype(BF16)
    place = jnp.zeros((LANES, HEAD_PAD), F32).at[jnp.arange(QK_ROPE), QK_NOPE + jnp.arange(QK_ROPE)].set(1.0)
    w["e_rope"] = jnp.tile(place, (1, N_HEADS)).astype(BF16)

    lam_re, lam_im, bb_re, bb_im = _ssm_prep(a_re, a_im, log_step, b_re, b_im)
    w["lam_re"] = lam_re.reshape(1, N_STATE)
    w["lam_im"] = lam_im.reshape(1, N_STATE)
    gpc = SSM_CH // SSM_GROUP
    eye = jnp.eye(gpc, dtype=F32)
    by_chunk = lambda m: m.reshape(N_SSM_CHUNKS, gpc, SSM_GROUP, SSM_STATE)
    blk = lambda m: jnp.einsum("cgpn,gh->cgphn", by_chunk(m), eye).reshape(N_SSM_CHUNKS, SSM_CH, SSM_ST)
    w["b_chunks"] = jnp.concatenate([blk(bb_re), blk(bb_im)], axis=2).astype(BF16)
    blk_t = lambda m: jnp.einsum("cgpn,gh->cgnhp", by_chunk(m), eye).reshape(N_SSM_CHUNKS, SSM_ST, SSM_CH)
    w["c_chunks"] = jnp.concatenate([blk_t(c_re), blk_t(-c_im)], axis=1).astype(BF16)
    w["d_skip"] = d_skip[None, :]
    w["w_glu"] = w_glu.astype(BF16)
    w["g_ssm_out"] = g_ssm_out[None, :]

    w["g_attn_out"] = g_attn_out[None, :]
    w["w_out_attn"] = w_out[:D_ATTN].astype(BF16)
    w["w_out_ssm"] = w_out[D_ATTN:].astype(BF16)
    w["g_mlp"] = g_mlp[None, :]
    w["w_up"] = w_up.astype(BF16)
    w["w_down"] = w_down.astype(BF16)
    w["g_final"] = g_final[None, :]
    return w


def _pack_state(h_re, h_im):
    nb = h_re.shape[0]
    return jnp.concatenate([h_re.reshape(nb, N_STATE), h_im.reshape(nb, N_STATE)], axis=1)


def _unpack_state(h):
    nb = h.shape[0]
    return (h[:, :N_STATE].reshape(nb, N_GROUPS, SSM_STATE), h[:, N_STATE:].reshape(nb, N_GROUPS, SSM_STATE))


def _branch(x, pos, past, h0, w, *, proj_tm, post_tm, ssm_tt):
    b, t, _ = x.shape
    n = b * t
    x2d = x.reshape(n, D_MODEL)
    tabs = _rope_tables(pos)
    if t < proj_tm:
        assert proj_tm % t == 0
        tabs = tuple(jnp.tile(a, (proj_tm // t, 1)) for a in tabs)
    else:
        assert t % proj_tm == 0
    if past is None:
        tq = min(ATTN_TQ, t)
        q, lat, kr, u, kpad, vt = _proj(x2d, tabs, w, proj_tm, t, tq)
        attn = _attn_prompt(q.reshape(b, t, D_HEADS_PAD), kpad.reshape(b, t, D_HEADS_PAD), vt, tq)
    else:
        q, lat, kr, u = _proj(x2d, tabs, w, proj_tm, t, None)
        past_lat, past_kr = past
        plen = past_lat.shape[1]
        assert plen % CHUNK == 0 and t <= CHUNK
        lane_pad = lambda a: jnp.pad(a, ((0, 0), (0, 0), (0, LANES - QK_ROPE))).astype(BF16)
        attn = _attn_sample(q.reshape(b, t, D_HEADS_PAD), past_lat, lane_pad(past_kr),
                            lat.reshape(b, t, KV_LORA), lane_pad(kr.reshape(b, t, QK_ROPE)), w,
                            min(ATTN_TK_SAMPLE, plen))
    u_tb = jnp.swapaxes(u.reshape(b, t, D_SSM), 0, 1)
    y_tb, h_fin = _ssm(u_tb, h0, w, min(ssm_tt, t))
    ssm2d = jnp.swapaxes(y_tb, 0, 1).reshape(n, D_SSM)
    y = _post(x2d, attn.reshape(n, D_ATTN), ssm2d, w, post_tm)
    h_re, h_im = _unpack_state(h_fin)
    return (y.reshape(b, t, D_MODEL), lat.reshape(1, b, t, KV_LORA), kr.reshape(1, b, t, QK_ROPE),
            h_re[None], h_im[None])


def kernel(x_prompt, x_sample, cache_kv_latent, cache_k_rope, state_ssm_re, state_ssm_im, g_mix, w_in, g_q_a,
           w_q_up, g_kv_a, w_kv_up, a_re, a_im, log_step, b_re, b_im, c_re, c_im, d_skip, w_glu, g_attn_out,
           g_ssm_out, w_out, g_mlp, w_up, w_down, g_final):
    assert g_mix.shape[0] == 1, "single-layer trunk"
    w = _layer_weights(g_mix[0], w_in[0], g_q_a[0], w_q_up[0], g_kv_a[0], w_kv_up[0], a_re[0], a_im[0],
                       log_step[0], b_re[0], b_im[0], c_re[0], c_im[0], d_skip[0], w_glu[0], g_attn_out[0],
                       g_ssm_out[0], w_out[0], g_mlp[0], w_up[0], w_down[0], g_final)
    tiles = dict(proj_tm=PROJ_TM, post_tm=POST_TM, ssm_tt=SSM_TT)

    bp, tp, _ = x_prompt.shape
    pos_p = jnp.arange(tp, dtype=jnp.int32)
    h0p = jnp.zeros((bp, 2 * N_STATE), F32)
    y_p, lat_p, kr_p, hr_p, hi_p = _branch(x_prompt, pos_p, None, h0p, w, **tiles)

    bs, ts, _ = x_sample.shape
    plen = cache_kv_latent.shape[2]
    pos_s = plen + jnp.arange(ts, dtype=jnp.int32)
    h0s = _pack_state(state_ssm_re[0], state_ssm_im[0])
    y_s, lat_s, kr_s, hr_s, hi_s = _branch(x_sample, pos_s, (cache_kv_latent[0], cache_k_rope[0]), h0s, w, **tiles)
    return (y_p, y_s, lat_p, kr_p, hr_p, hi_p, lat_s, kr_s, hr_s, hi_s)
```

```python
import functools
import math

import jax
import jax.numpy as jnp
from jax import lax
from jax.experimental import pallas as pl
from jax.experimental.pallas import tpu as pltpu

F32 = jnp.float32
BF16 = jnp.bfloat16

D_MODEL = 1024
N_HEADS = 8
QK_NOPE = 64
QK_ROPE = 32
ROPE_HALF = QK_ROPE // 2
V_DIM = 64
KV_LORA = 256
Q_LORA = 768
D_ATTN = N_HEADS * V_DIM
D_SSM = 512
SSM_GROUP = 16
N_GROUPS = D_SSM // SSM_GROUP
SSM_STATE = 64
N_STATE = N_GROUPS * SSM_STATE
D_FF = 4 * D_MODEL
CHUNK = 64
ROPE_THETA = 10000.0
SOFTMAX_SCALE = (QK_NOPE + QK_ROPE) ** -0.5
LOG2_E = math.log2(math.e)
EPS = 1e-6
NEG_INF = -1e30

LANES = 128
SUBLANES = 8
HEAD_PAD = LANES
D_HEADS_PAD = N_HEADS * HEAD_PAD
COL_CQ = 0
COL_CKV = Q_LORA
COL_U = Q_LORA + KV_LORA
COL_KPE = COL_U + D_SSM
COL_KPE_ROT = COL_KPE + LANES
D_IN_EXT = COL_KPE_ROT + LANES

VMEM_LIMIT = 56 * 1024 * 1024

PROJ_TM = 512
ATTN_TQ = 512
ATTN_TK_SAMPLE = 512
ATTN_SUB = 2
SSM_TT = 64
POST_TM = 512
FF_CHUNK = 1024


def _const_spec(shape):
    nd = len(shape)
    return pl.BlockSpec(shape, lambda *_: (0,) * nd, pipeline_mode=pl.Buffered(1))


def _rms(x, g):
    return x * lax.rsqrt(jnp.mean(x * x, axis=-1, keepdims=True) + EPS) * g


def _prep_kernel(are_ref, aim_ref, ls_ref, bre_ref, bim_ref, lre_ref, lim_ref, bbre_ref, bbim_ref):
    dt = jnp.exp(ls_ref[...])
    lr, li = are_ref[...], aim_ref[...]
    mag = jnp.exp(lr * dt)
    lb_re, lb_im = mag * jnp.cos(li * dt), mag * jnp.sin(li * dt)
    nr, ni = lb_re - 1.0, lb_im
    den = lr * lr + li * li
    coef_re = (nr * lr + ni * li) / den
    coef_im = (ni * lr - nr * li) / den
    lre_ref[...] = lb_re
    lim_ref[...] = lb_im
    bre, bim = bre_ref[...], bim_ref[...]
    cr, ci = coef_re[:, None, :], coef_im[:, None, :]
    bbre_ref[...] = cr * bre - ci * bim
    bbim_ref[...] = cr * bim + ci * bre


def _ssm_prep(a_re, a_im, log_step, b_re, b_im):
    g, n = a_re.shape
    p = b_re.shape[-1]
    bre_t = jnp.swapaxes(b_re, 1, 2)
    bim_t = jnp.swapaxes(b_im, 1, 2)
    return pl.pallas_call(
        _prep_kernel,
        out_shape=(jax.ShapeDtypeStruct((g, n), F32), jax.ShapeDtypeStruct((g, n), F32),
                   jax.ShapeDtypeStruct((g, p, n), F32), jax.ShapeDtypeStruct((g, p, n), F32)),
        name="ssm_prep",
    )(a_re, a_im, log_step.reshape(g, 1), bre_t, bim_t)


def _proj_kernel(x_ref, cq_ref, sqa_ref, sqb_ref, ck_ref, sk_ref, gmix_ref, win_ref, gq_ref, wq_ref, gkv_ref,
                 wk_ref, wvt_ref,
                 q_out, lat_out, kr_out, u_out, *kv_out, v_key_tile):
    xn = _rms(x_ref[...], gmix_ref[...]).astype(BF16)
    proj = jnp.dot(xn, win_ref[...], preferred_element_type=F32)
    u_out[...] = proj[:, COL_U:COL_U + D_SSM]

    cqn = _rms(proj[:, COL_CQ:COL_CQ + Q_LORA], gq_ref[...]).astype(BF16)
    q = jnp.dot(cqn, wq_ref[...], preferred_element_type=F32)
    q_up = pltpu.roll(q, ROPE_HALF, 1)
    q_dn = pltpu.roll(q, D_HEADS_PAD - ROPE_HALF, 1)
    cq_t, sqa_t, sqb_t = cq_ref[...], sqa_ref[...], sqb_ref[...]
    for h in range(N_HEADS):
        sl = slice(h * HEAD_PAD, (h + 1) * HEAD_PAD)
        q_out[:, sl] = (q[:, sl] * cq_t + q_up[:, sl] * sqa_t + q_dn[:, sl] * sqb_t).astype(BF16)

    lat = _rms(proj[:, COL_CKV:COL_CKV + KV_LORA], gkv_ref[...])
    lat_out[...] = lat
    kr = (proj[:, COL_KPE:COL_KPE + LANES] * ck_ref[...]
          + proj[:, COL_KPE_ROT:COL_KPE_ROT + LANES] * sk_ref[...])
    kr_out[...] = kr[:, :QK_ROPE]
    if v_key_tile is None:
        return
    kpad_out, vt_out = kv_out
    latb = lat.astype(BF16)
    k_nope = jnp.dot(latb, wk_ref[...], preferred_element_type=F32)
    kr_placed = pltpu.roll(kr, QK_NOPE, 1)
    for h in range(N_HEADS):
        sl = slice(h * HEAD_PAD, (h + 1) * HEAD_PAD)
        kpad_out[:, sl] = (k_nope[:, sl] + kr_placed).astype(BF16)
    vt = lax.dot_general(wvt_ref[...], latb, (((1,), (1,)), ((), ())),
                         preferred_element_type=F32)
    ones = lax.broadcasted_iota(jnp.int32, vt.shape, 0) % HEAD_PAD >= V_DIM
    vt = jnp.where(ones, 1.0, vt).astype(BF16)
    for s in range(vt.shape[1] // v_key_tile):
        vt_out[s] = vt[:, s * v_key_tile:(s + 1) * v_key_tile]


def _proj(x2d, tabs, w, tm, stream_len, v_key_tile):
    n = x2d.shape[0]
    cosq, sinq_up, sinq_dn, cosk, sink = tabs
    n_tab = cosq.shape[0] // tm
    row = lambda i: (i, 0)
    tab = lambda i: (i % n_tab, 0)
    out_shape = [
        jax.ShapeDtypeStruct((n, D_HEADS_PAD), BF16),
        jax.ShapeDtypeStruct((n, KV_LORA), F32),
        jax.ShapeDtypeStruct((n, QK_ROPE), F32),
        jax.ShapeDtypeStruct((n, D_SSM), F32),
    ]
    out_specs = [pl.BlockSpec((tm, D_HEADS_PAD), row), pl.BlockSpec((tm, KV_LORA), row),
                 pl.BlockSpec((tm, QK_ROPE), row), pl.BlockSpec((tm, D_SSM), row)]
    if v_key_tile is not None:
        assert stream_len % tm == 0 and tm % v_key_tile == 0
        tiles_per_stream = stream_len // tm
        out_shape += [jax.ShapeDtypeStruct((n, D_HEADS_PAD), BF16),
                      jax.ShapeDtypeStruct((n // stream_len, stream_len // v_key_tile, D_HEADS_PAD, v_key_tile),
                                           BF16)]
        out_specs += [pl.BlockSpec((tm, D_HEADS_PAD), row),
                      pl.BlockSpec((None, tm // v_key_tile, D_HEADS_PAD, v_key_tile),
                                   lambda i: (i // tiles_per_stream, i % tiles_per_stream, 0, 0))]
    return pl.pallas_call(
        functools.partial(_proj_kernel, v_key_tile=v_key_tile),
        grid=(n // tm,),
        in_specs=[
            pl.BlockSpec((tm, D_MODEL), row),
            pl.BlockSpec((tm, LANES), tab), pl.BlockSpec((tm, LANES), tab), pl.BlockSpec((tm, LANES), tab),
            pl.BlockSpec((tm, LANES), tab), pl.BlockSpec((tm, LANES), tab),
            _const_spec((1, D_MODEL)), _const_spec((D_MODEL, D_IN_EXT)),
            _const_spec((1, Q_LORA)), _const_spec((Q_LORA, D_HEADS_PAD)),
            _const_spec((1, KV_LORA)), _const_spec((KV_LORA, D_HEADS_PAD)),
            _const_spec((D_HEADS_PAD, KV_LORA)),
        ],
        out_specs=tuple(out_specs),
        out_shape=tuple(out_shape),
        compiler_params=pltpu.CompilerParams(dimension_semantics=("parallel",),
                                             vmem_limit_bytes=VMEM_LIMIT),
        name="proj",
    )(x2d, cosq, sinq_up, sinq_dn, cosk, sink, w["g_mix"], w["w_in_ext"], w["g_q_a"], w["w_q_pad"], w["g_kv_a"],
      w["w_k_pad"], w["w_v_t"])


def _attn_prompt_kernel(q_ref, k_ref, vt_ref, o_ref, s00, s01, s10, s11, *, tq):
    i = pl.program_id(2)
    key_chunk = lax.broadcasted_iota(jnp.int32, (tq, tq), 0) // CHUNK
    query_chunk = lax.broadcasted_iota(jnp.int32, (tq, tq), 1) // CHUNK
    diag_mask = key_chunk <= query_chunk
    heads = [slice(h * HEAD_PAD, (h + 1) * HEAD_PAD) for h in range(2)]
    slots = ((s00, s01), (s10, s11))
    n_sub = ATTN_SUB
    ts = tq // n_sub

    def scores(j, slot):
        off = pl.multiple_of(j * tq, tq)
        for sl, buf in zip(heads, slots[slot]):
            buf[...] = lax.dot_general(k_ref[pl.ds(off, tq), sl], q_ref[:, sl], (((1,), (1,)), ((), ())),
                                       preferred_element_type=F32)

    def consume(j, slot, state, mask):
        for s in range(n_sub):
            rows = slice(s * ts, (s + 1) * ts)
            new_state = []
            for sl, buf, (m, acc) in zip(heads, slots[slot], state):
                st = buf[rows, :]
                if mask is not None:
                    st = jnp.where(mask[rows], st, NEG_INF)
                m_new = jnp.maximum(m, jnp.max(st, axis=0, keepdims=True))
                alpha = jnp.exp2(m - m_new)
                p = jnp.exp2(st - m_new).astype(BF16)
                acc = acc * alpha + jnp.dot(vt_ref[j, sl, rows], p, preferred_element_type=F32)
                new_state.append((m_new, acc))
            state = tuple(new_state)
        return state

    def pair(p, state):
        scores(2 * p + 1, 1)
        state = consume(2 * p, 0, state, None)
        scores(2 * p + 2, 0)
        return consume(2 * p + 1, 1, state, None)

    init = tuple((jnp.full((1, tq), NEG_INF, F32), jnp.zeros((HEAD_PAD, tq), F32)) for _ in heads)
    scores(0, 0)
    state = lax.fori_loop(0, i // 2, pair, init)

    def even_tail(state):
        return consume(i, 0, state, diag_mask)

    def odd_tail(state):
        scores(i, 1)
        state = consume(i - 1, 0, state, None)
        return consume(i, 1, state, diag_mask)

    state = lax.cond(i % 2 == 0, even_tail, odd_tail, state)
    o_t = jnp.concatenate([acc[:V_DIM] / acc[V_DIM:V_DIM + 1] for _, acc in state], axis=0)
    o_ref[...] = o_t.T


def _attn_prompt(q, kpad, vt, tq):
    b, t, _ = q.shape
    assert tq % CHUNK == 0 and t % tq == 0 and vt.shape == (b, t // tq, D_HEADS_PAD, tq)
    return pl.pallas_call(
        functools.partial(_attn_prompt_kernel, tq=tq),
        grid=(b, N_HEADS // 2, t // tq),
        in_specs=[pl.BlockSpec((None, tq, 2 * HEAD_PAD), lambda bi, hp, i: (bi, i, hp)),
                  pl.BlockSpec((None, t, 2 * HEAD_PAD), lambda bi, hp, i: (bi, 0, hp)),
                  pl.BlockSpec((None, t // tq, 2 * HEAD_PAD, tq), lambda bi, hp, i: (bi, 0, hp, 0))],
        out_specs=pl.BlockSpec((None, tq, 2 * V_DIM), lambda bi, hp, i: (bi, i, hp)),
        out_shape=jax.ShapeDtypeStruct((b, t, D_ATTN), F32),
        scratch_shapes=[pltpu.VMEM((tq, tq), F32)] * 4,
        compiler_params=pltpu.CompilerParams(
            dimension_semantics=("parallel", "parallel", "arbitrary"), vmem_limit_bytes=VMEM_LIMIT),
        name="attn_prompt",
    )(q, kpad, vt)


D_KCAT = KV_LORA + LANES


def _attn_sample_kernel(q_ref, plat_ref, pkr_ref, nlat_ref, nkr_ref, wabs_ref, wv_ref, o_ref, *, tk):
    tq = q_ref.shape[0]
    n_past = plat_ref.shape[0] // tk
    qcat = jnp.concatenate(
        [jnp.dot(q_ref[:, h * HEAD_PAD:(h + 1) * HEAD_PAD], wabs_ref[h], preferred_element_type=F32)
         for h in range(N_HEADS)], axis=0).astype(BF16)

    def step(lat, kr, carry):
        m, l, acc = carry
        latb = lat.astype(BF16)
        kcat = jnp.concatenate([latb, kr], axis=1)
        s = lax.dot_general(qcat, kcat, (((1,), (1,)), ((), ())), preferred_element_type=F32)
        m_new = jnp.maximum(m, jnp.max(s, axis=-1, keepdims=True))
        alpha = jnp.exp2(m - m_new)
        p = jnp.exp2(s - m_new)
        l = l * alpha + jnp.sum(p, axis=-1, keepdims=True)
        acc = acc * alpha + jnp.dot(p.astype(BF16), latb, preferred_element_type=F32)
        return m_new, l, acc

    def body(j, carry):
        off = pl.multiple_of(j * tk, tk)
        return step(plat_ref[pl.ds(off, tk), :], pkr_ref[pl.ds(off, tk), :], carry)

    rows = N_HEADS * tq
    init = (jnp.full((rows, 1), NEG_INF, F32), jnp.zeros((rows, 1), F32), jnp.zeros((rows, KV_LORA), F32))
    carry = lax.fori_loop(0, n_past, body, init)
    _, l, acc = step(nlat_ref[...], nkr_ref[...], carry)
    o_lat = (acc / l).astype(BF16)
    out = jnp.zeros((tq, D_ATTN), F32)
    for h in range(N_HEADS):
        out = out + jnp.dot(o_lat[h * tq:(h + 1) * tq], wv_ref[h], preferred_element_type=F32)
    o_ref[...] = out


def _attn_sample(q, past_lat, past_kr, new_lat, new_kr, w, tk):
    b, tq, _ = q.shape
    past = past_lat.shape[1]
    assert past % tk == 0
    blk = lambda bi: (bi, 0, 0)
    return pl.pallas_call(
        functools.partial(_attn_sample_kernel, tk=tk),
        grid=(b,),
        in_specs=[pl.BlockSpec((None, tq, D_HEADS_PAD), blk),
                  pl.BlockSpec((None, past, KV_LORA), blk), pl.BlockSpec((None, past, LANES), blk),
                  pl.BlockSpec((None, tq, KV_LORA), blk), pl.BlockSpec((None, tq, LANES), blk),
                  _const_spec((N_HEADS, HEAD_PAD, D_KCAT)), _const_spec((N_HEADS, KV_LORA, D_ATTN))],
        out_specs=pl.BlockSpec((None, tq, D_ATTN), blk),
        out_shape=jax.ShapeDtypeStruct((b, tq, D_ATTN), F32),
        compiler_params=pltpu.CompilerParams(dimension_semantics=("parallel",),
                                             vmem_limit_bytes=VMEM_LIMIT),
        name="attn_sample",
    )(q, past_lat, past_kr, new_lat, new_kr, w["w_q_abs"], w["w_v_heads"])


SSM_CH = LANES
SSM_ST = SSM_CH // SSM_GROUP * SSM_STATE
N_SSM_CHUNKS = D_SSM // SSM_CH


def _ssm_kernel(u_ref, h0_ref, lre_ref, lim_ref, bc_ref, cc_ref, d_ref, wglu_ref, gssm_ref,
                y_ref, hout_ref, x_s, h_s, hc_s, *, tt):
    ti = pl.program_id(1)

    @pl.when(ti == 0)
    def _():
        hc_s[...] = h0_ref[...]

    rows = tt * SUBLANES
    u = u_ref[...].reshape(rows, D_SSM)
    ub = u.astype(BF16)
    chunks = []
    for c in range(N_SSM_CHUNKS):
        re = slice(c * SSM_ST, (c + 1) * SSM_ST)
        im = slice(N_STATE + c * SSM_ST, N_STATE + (c + 1) * SSM_ST)
        chunks.append((re, im))
        xc = jnp.dot(ub[:, c * SSM_CH:(c + 1) * SSM_CH], bc_ref[c], preferred_element_type=F32)
        x_s[:, :, re] = xc[:, :SSM_ST].reshape(tt, SUBLANES, SSM_ST)
        x_s[:, :, im] = xc[:, SSM_ST:].reshape(tt, SUBLANES, SSM_ST)

    for re, im in chunks:
        lr = jnp.broadcast_to(lre_ref[:, re], (SUBLANES, SSM_ST))
        li = jnp.broadcast_to(lim_ref[:, re], (SUBLANES, SSM_ST))

        def step(t, carry, re=re, im=im, lr=lr, li=li):
            hr, hi = carry
            nhr = lr * hr - li * hi + x_s[t, :, re]
            nhi = lr * hi + li * hr + x_s[t, :, im]
            h_s[t, :, re] = nhr
            h_s[t, :, im] = nhi
            return nhr, nhi

        hr, hi = lax.fori_loop(0, tt, step, (hc_s[:, re], hc_s[:, im]), unroll=True)
        hc_s[:, re] = hr
        hc_s[:, im] = hi

    ys = []
    for c, (re, im) in enumerate(chunks):
        hcat = jnp.concatenate([h_s[:, :, re], h_s[:, :, im]], axis=-1).reshape(rows, 2 * SSM_ST)
        ys.append(jnp.dot(hcat.astype(BF16), cc_ref[c], preferred_element_type=F32))
    y = jnp.concatenate(ys, axis=1) + d_ref[...] * u
    y = jax.nn.gelu(y, approximate=True)
    z = jnp.dot(y.astype(BF16), wglu_ref[...], preferred_element_type=F32)
    y = y * (1.0 / (1.0 + jnp.exp(-z)))
    y_ref[...] = _rms(y, gssm_ref[...]).reshape(tt, SUBLANES, D_SSM)

    @pl.when(ti == pl.num_programs(1) - 1)
    def _():
        hout_ref[...] = hc_s[...]


def _ssm(u_tb, h0, w, tt):
    t, nb, _ = u_tb.shape
    assert nb % SUBLANES == 0 and t % tt == 0
    return pl.pallas_call(
        functools.partial(_ssm_kernel, tt=tt),
        grid=(nb // SUBLANES, t // tt),
        in_specs=[pl.BlockSpec((tt, SUBLANES, D_SSM), lambda bg, ti: (ti, bg, 0)),
                  pl.BlockSpec((SUBLANES, 2 * N_STATE), lambda bg, ti: (bg, 0)),
                  _const_spec((1, N_STATE)), _const_spec((1, N_STATE)),
                  _const_spec((N_SSM_CHUNKS, SSM_CH, 2 * SSM_ST)), _const_spec((N_SSM_CHUNKS, 2 * SSM_ST, SSM_CH)),
                  _const_spec((1, D_SSM)), _const_spec((D_SSM, D_SSM)), _const_spec((1, D_SSM))],
        out_specs=(pl.BlockSpec((tt, SUBLANES, D_SSM), lambda bg, ti: (ti, bg, 0)),
                   pl.BlockSpec((SUBLANES, 2 * N_STATE), lambda bg, ti: (bg, 0))),
        out_shape=(jax.ShapeDtypeStruct((t, nb, D_SSM), F32),
                   jax.ShapeDtypeStruct((nb, 2 * N_STATE), F32)),
        scratch_shapes=[pltpu.VMEM((tt, SUBLANES, 2 * N_STATE), F32),
                        pltpu.VMEM((tt, SUBLANES, 2 * N_STATE), F32),
                        pltpu.VMEM((SUBLANES, 2 * N_STATE), F32)],
        compiler_params=pltpu.CompilerParams(dimension_semantics=("parallel", "arbitrary"),
                                             vmem_limit_bytes=VMEM_LIMIT),
        name="ssm",
    )(u_tb, h0, w["lam_re"], w["lam_im"], w["b_chunks"], w["c_chunks"], w["d_skip"], w["w_glu"],
      w["g_ssm_out"])


def _post_kernel(x_ref, attn_ref, ssm_ref, gattn_ref, wouta_ref, wouts_ref, gmlp_ref, wup_ref, wdown_ref,
                 gfin_ref, y_ref):
    an = _rms(attn_ref[...], gattn_ref[...]).astype(BF16)
    mixed = (jnp.dot(an, wouta_ref[...], preferred_element_type=F32)
             + jnp.dot(ssm_ref[...].astype(BF16), wouts_ref[...], preferred_element_type=F32))
    h = x_ref[...] + mixed
    hn = _rms(h, gmlp_ref[...]).astype(BF16)
    acc = jnp.zeros(h.shape, F32)
    for c in range(D_FF // FF_CHUNK):
        ff = slice(c * FF_CHUNK, (c + 1) * FF_CHUNK)
        a = jnp.dot(hn, wup_ref[:, ff], preferred_element_type=F32)
        a = jnp.square(jnp.maximum(a, 0.0))
        acc = acc + jnp.dot(a.astype(BF16), wdown_ref[ff, :], preferred_element_type=F32)
    y_ref[...] = _rms(h + acc, gfin_ref[...])


def _post(x2d, attn2d, ssm2d, w, tm):
    n = x2d.shape[0]
    row = lambda i: (i, 0)
    return pl.pallas_call(
        _post_kernel,
        grid=(n // tm,),
        in_specs=[pl.BlockSpec((tm, D_MODEL), row), pl.BlockSpec((tm, D_ATTN), row),
                  pl.BlockSpec((tm, D_SSM), row),
                  _const_spec((1, D_ATTN)), _const_spec((D_ATTN, D_MODEL)), _const_spec((D_SSM, D_MODEL)),
                  _const_spec((1, D_MODEL)), _const_spec((D_MODEL, D_FF)), _const_spec((D_FF, D_MODEL)),
                  _const_spec((1, D_MODEL))],
        out_specs=pl.BlockSpec((tm, D_MODEL), row),
        out_shape=jax.ShapeDtypeStruct((n, D_MODEL), F32),
        compiler_params=pltpu.CompilerParams(dimension_semantics=("parallel",),
                                             vmem_limit_bytes=VMEM_LIMIT),
        name="post",
    )(x2d, attn2d, ssm2d, w["g_attn_out"], w["w_out_attn"], w["w_out_ssm"], w["g_mlp"], w["w_up"],
      w["w_down"], w["g_final"])


def _rope_tables(pos):
    t = pos.shape[0]
    inv_freq = ROPE_THETA ** (-(jnp.arange(ROPE_HALF, dtype=F32) * 2.0) / QK_ROPE)
    ang = pos.astype(F32)[:, None] * inv_freq[None, :]
    cc = jnp.tile(jnp.cos(ang), (1, 2))
    ss = jnp.tile(jnp.sin(ang), (1, 2))
    pad_hi = jnp.zeros((t, HEAD_PAD - QK_NOPE - QK_ROPE), F32)
    qs = SOFTMAX_SCALE * LOG2_E
    cosq = jnp.concatenate([jnp.full((t, QK_NOPE), qs, F32), cc * qs, pad_hi], axis=1)
    z_half = jnp.zeros((t, ROPE_HALF), F32)
    sin_h = jnp.sin(ang) * qs
    sinq_up = jnp.concatenate([jnp.zeros((t, QK_NOPE), F32), z_half, sin_h, pad_hi], axis=1)
    sinq_dn = jnp.concatenate([jnp.zeros((t, QK_NOPE), F32), -sin_h, z_half, pad_hi], axis=1)
    pad_k = jnp.zeros((t, LANES - QK_ROPE), F32)
    cosk = jnp.concatenate([cc, pad_k], axis=1)
    sink = jnp.concatenate([ss, pad_k], axis=1)
    return cosq, sinq_up, sinq_dn, cosk, sink


def _rot_cols(w_x1, w_x2):
    return -w_x2, w_x1


def _layer_weights(g_mix, w_in, g_q_a, w_q_up, g_kv_a, w_kv_up, a_re, a_im, log_step, b_re, b_im,
                   c_re, c_im, d_skip, w_glu, g_attn_out, g_ssm_out, w_out, g_mlp, w_up, w_down, g_final):
    w = {}
    w["g_mix"] = g_mix[None, :]
    w_cq, w_ckv = w_in[:, :Q_LORA], w_in[:, Q_LORA:Q_LORA + KV_LORA]
    w_kpe = w_in[:, Q_LORA + KV_LORA:Q_LORA + KV_LORA + QK_ROPE]
    w_u = w_in[:, Q_LORA + KV_LORA + QK_ROPE:]
    rot1, rot2 = _rot_cols(w_kpe[:, :ROPE_HALF], w_kpe[:, ROPE_HALF:])
    zk = jnp.zeros((D_MODEL, LANES - QK_ROPE), F32)
    w["w_in_ext"] = jnp.concatenate([w_cq, w_ckv, w_u, w_kpe, zk, rot1, rot2, zk], axis=1).astype(BF16)

    w["g_q_a"] = g_q_a[None, :]
    wq = w_q_up.reshape(Q_LORA, N_HEADS, QK_NOPE + QK_ROPE)
    nope, r1, r2 = wq[:, :, :QK_NOPE], wq[:, :, QK_NOPE:QK_NOPE + ROPE_HALF], wq[:, :, QK_NOPE + ROPE_HALF:]
    zq = jnp.zeros((Q_LORA, N_HEADS, HEAD_PAD - QK_NOPE - QK_ROPE), F32)
    w["w_q_pad"] = jnp.concatenate([nope, r1, r2, zq], axis=2).reshape(Q_LORA, D_HEADS_PAD).astype(BF16)

    w["g_kv_a"] = g_kv_a[None, :]
    wkv = w_kv_up.reshape(KV_LORA, N_HEADS, QK_NOPE + V_DIM)
    zkv = jnp.zeros((KV_LORA, N_HEADS, HEAD_PAD - QK_NOPE), F32)
    wk_pad = jnp.concatenate([wkv[:, :, :QK_NOPE], zkv], axis=2).reshape(KV_LORA, D_HEADS_PAD)
    wv_pad = jnp.concatenate([wkv[:, :, QK_NOPE:], zkv], axis=2).reshape(KV_LORA, D_HEADS_PAD)
    w["w_k_pad"] = wk_pad.astype(BF16)
    w["w_v_t"] = wv_pad.T.astype(BF16)
    wk_t = jnp.transpose(wkv[:, :, :QK_NOPE], (1, 2, 0))
    q_abs = jnp.zeros((N_HEADS, HEAD_PAD, D_KCAT), F32).at[:, :QK_NOPE, :KV_LORA].set(wk_t)
    q_abs = q_abs.at[:, QK_NOPE + jnp.arange(QK_ROPE), KV_LORA + jnp.arange(QK_ROPE)].set(1.0)
    w["w_q_abs"] = q_abs.astype(BF16)
    wv_h = jnp.transpose(wkv[:, :, QK_NOPE:], (1, 0, 2))
    w["w_v_heads"] = jnp.einsum("hce,hg->hcge", wv_h, jnp.eye(N_HEADS, dtype=F32)).reshape(
        N_HEADS, KV_LORA, D_ATTN).astype(BF16)

    lam_re, lam_im, bb_re, bb_im = _ssm_prep(a_re, a_im, log_step, b_re, b_im)
    w["lam_re"] = lam_re.reshape(1, N_STATE)
    w["lam_im"] = lam_im.reshape(1, N_STATE)
    gpc = SSM_CH // SSM_GROUP
    eye = jnp.eye(gpc, dtype=F32)
    by_chunk = lambda m: m.reshape(N_SSM_CHUNKS, gpc, SSM_GROUP, SSM_STATE)
    blk = lambda m: jnp.einsum("cgpn,gh->cgphn", by_chunk(m), eye).reshape(N_SSM_CHUNKS, SSM_CH, SSM_ST)
    w["b_chunks"] = jnp.concatenate([blk(bb_re), blk(bb_im)], axis=2).astype(BF16)
    blk_t = lambda m: jnp.einsum("cgpn,gh->cgnhp", by_chunk(m), eye).reshape(N_SSM_CHUNKS, SSM_ST, SSM_CH)
    w["c_chunks"] = jnp.concatenate([blk_t(c_re), blk_t(-c_im)], axis=1).astype(BF16)
    w["d_skip"] = d_skip[None, :]
    w["w_glu"] = w_glu.astype(BF16)
    w["g_ssm_out"] = g_ssm_out[None, :]

    w["g_attn_out"] = g_attn_out[None, :]
    w["w_out_attn"] = w_out[:D_ATTN].astype(BF16)
    w["w_out_ssm"] = w_out[D_ATTN:].astype(BF16)
    w["g_mlp"] = g_mlp[None, :]
    w["w_up"] = w_up.astype(BF16)
    w["w_down"] = w_down.astype(BF16)
    w["g_final"] = g_final[None, :]
    return w


def _pack_state(h_re, h_im):
    nb = h_re.shape[0]
    return jnp.concatenate([h_re.reshape(nb, N_STATE), h_im.reshape(nb, N_STATE)], axis=1)


def _unpack_state(h):
    nb = h.shape[0]
    return (h[:, :N_STATE].reshape(nb, N_GROUPS, SSM_STATE), h[:, N_STATE:].reshape(nb, N_GROUPS, SSM_STATE))


def _branch(x, pos, past, h0, w, *, proj_tm, post_tm, ssm_tt):
    b, t, _ = x.shape
    n = b * t
    x2d = x.reshape(n, D_MODEL)
    tabs = _rope_tables(pos)
    if t < proj_tm:
        assert proj_tm % t == 0
        tabs = tuple(jnp.tile(a, (proj_tm // t, 1)) for a in tabs)
    else:
        assert t % proj_tm == 0
    if past is None:
        tq = min(ATTN_TQ, t)
        q, lat, kr, u, kpad, vt = _proj(x2d, tabs, w, proj_tm, t, tq)
        attn = _attn_prompt(q.reshape(b, t, D_HEADS_PAD), kpad.reshape(b, t, D_HEADS_PAD), vt, tq)
    else:
        q, lat, kr, u = _proj(x2d, tabs, w, proj_tm, t, None)
        past_lat, past_kr = past
        plen = past_lat.shape[1]
        assert plen % CHUNK == 0 and t <= CHUNK
        lane_pad = lambda a: jnp.pad(a, ((0, 0), (0, 0), (0, LANES - QK_ROPE))).astype(BF16)
        attn = _attn_sample(q.reshape(b, t, D_HEADS_PAD), past_lat, lane_pad(past_kr),
                            lat.reshape(b, t, KV_LORA), lane_pad(kr.reshape(b, t, QK_ROPE)), w,
                            min(ATTN_TK_SAMPLE, plen))
    u_tb = jnp.swapaxes(u.reshape(b, t, D_SSM), 0, 1)
    y_tb, h_fin = _ssm(u_tb, h0, w, min(ssm_tt, t))
    ssm2d = jnp.swapaxes(y_tb, 0, 1).reshape(n, D_SSM)
    y = _post(x2d, attn.reshape(n, D_ATTN), ssm2d, w, post_tm)
    h_re, h_im = _unpack_state(h_fin)
    return (y.reshape(b, t, D_MODEL), lat.reshape(1, b, t, KV_LORA), kr.reshape(1, b, t, QK_ROPE),
            h_re[None], h_im[None])


def kernel(x_prompt, x_sample, cache_kv_latent, cache_k_rope, state_ssm_re, state_ssm_im, g_mix, w_in, g_q_a,
           w_q_up, g_kv_a, w_kv_up, a_re, a_im, log_step, b_re, b_im, c_re, c_im, d_skip, w_glu, g_attn_out,
           g_ssm_out, w_out, g_mlp, w_up, w_down, g_final):
    assert g_mix.shape[0] == 1, "single-layer trunk"
    w = _layer_weights(g_mix[0], w_in[0], g_q_a[0], w_q_up[0], g_kv_a[0], w_kv_up[0], a_re[0], a_im[0],
                       log_step[0], b_re[0], b_im[0], c_re[0], c_im[0], d_skip[0], w_glu[0], g_attn_out[0],
                       g_ssm_out[0], w_out[0], g_mlp[0], w_up[0], w_down[0], g_final)
    tiles = dict(proj_tm=PROJ_TM, post_tm=POST_TM, ssm_tt=SSM_TT)

    bp, tp, _ = x_prompt.shape
    pos_p = jnp.arange(tp, dtype=jnp.int32)
    h0p = jnp.zeros((bp, 2 * N_STATE), F32)
    y_p, lat_p, kr_p, hr_p, hi_p = _branch(x_prompt, pos_p, None, h0p, w, **tiles)

    bs, ts, _ = x_sample.shape
    plen = cache_kv_latent.shape[2]
    pos_s = plen + jnp.arange(ts, dtype=jnp.int32)
    h0s = _pack_state(state_ssm_re[0], state_ssm_im[0])
    y_s, lat_s, kr_s, hr_s, hi_s = _branch(x_sample, pos_s, (cache_kv_latent[0], cache_k_rope[0]), h0s, w, **tiles)
    return (y_p, y_s, lat_p, kr_p, hr_p, hi_p, lat_s, kr_s, hr_s, hi_s)
```

```python
import functools
import math

import jax
import jax.numpy as jnp
from jax import lax
from jax.experimental import pallas as pl
from jax.experimental.pallas import tpu as pltpu

F32 = jnp.float32
BF16 = jnp.bfloat16

D_MODEL = 1024
N_HEADS = 8
QK_NOPE = 64
QK_ROPE = 32
ROPE_HALF = QK_ROPE // 2
V_DIM = 64
KV_LORA = 256
Q_LORA = 768
D_ATTN = N_HEADS * V_DIM
D_SSM = 512
SSM_GROUP = 16
N_GROUPS = D_SSM // SSM_GROUP
SSM_STATE = 64
N_STATE = N_GROUPS * SSM_STATE
D_FF = 4 * D_MODEL
CHUNK = 64
ROPE_THETA = 10000.0
SOFTMAX_SCALE = (QK_NOPE + QK_ROPE) ** -0.5
LOG2_E = math.log2(math.e)
EPS = 1e-6
NEG_INF = -1e30

LANES = 128
SUBLANES = 8
HEAD_PAD = LANES
D_HEADS_PAD = N_HEADS * HEAD_PAD
COL_CQ = 0
COL_CKV = Q_LORA
COL_U = Q_LORA + KV_LORA
COL_KPE = COL_U + D_SSM
COL_KPE_ROT = COL_KPE + LANES
D_IN_EXT = COL_KPE_ROT + LANES

VMEM_LIMIT = 56 * 1024 * 1024

PROJ_TM = 512
ATTN_TQ = 512
ATTN_TK_SAMPLE = 512
ATTN_SUB = 2
SSM_TT = 64
POST_TM = 512
FF_CHUNK = 1024


def _const_spec(shape):
    nd = len(shape)
    return pl.BlockSpec(shape, lambda *_: (0,) * nd, pipeline_mode=pl.Buffered(1))


def _rms(x, g):
    return x * lax.rsqrt(jnp.mean(x * x, axis=-1, keepdims=True) + EPS) * g


def _prep_kernel(are_ref, aim_ref, ls_ref, bre_ref, bim_ref, lre_ref, lim_ref, bbre_ref, bbim_ref):
    dt = jnp.exp(ls_ref[...])
    lr, li = are_ref[...], aim_ref[...]
    mag = jnp.exp(lr * dt)
    lb_re, lb_im = mag * jnp.cos(li * dt), mag * jnp.sin(li * dt)
    nr, ni = lb_re - 1.0, lb_im
    den = lr * lr + li * li
    coef_re = (nr * lr + ni * li) / den
    coef_im = (ni * lr - nr * li) / den
    lre_ref[...] = lb_re
    lim_ref[...] = lb_im
    bre, bim = bre_ref[...], bim_ref[...]
    cr, ci = coef_re[:, None, :], coef_im[:, None, :]
    bbre_ref[...] = cr * bre - ci * bim
    bbim_ref[...] = cr * bim + ci * bre


def _ssm_prep(a_re, a_im, log_step, b_re, b_im):
    g, n = a_re.shape
    p = b_re.shape[-1]
    bre_t = jnp.swapaxes(b_re, 1, 2)
    bim_t = jnp.swapaxes(b_im, 1, 2)
    return pl.pallas_call(
        _prep_kernel,
        out_shape=(jax.ShapeDtypeStruct((g, n), F32), jax.ShapeDtypeStruct((g, n), F32),
                   jax.ShapeDtypeStruct((g, p, n), F32), jax.ShapeDtypeStruct((g, p, n), F32)),
        name="ssm_prep",
    )(a_re, a_im, log_step.reshape(g, 1), bre_t, bim_t)


def _proj_kernel(x_ref, cq_ref, sqa_ref, sqb_ref, ck_ref, sk_ref, gmix_ref, win_ref, gq_ref, wq_ref, gkv_ref,
                 wk_ref, wvt_ref,
                 q_out, lat_out, kr_out, u_out, *kv_out, v_key_tile):
    xn = _rms(x_ref[...], gmix_ref[...]).astype(BF16)
    proj = jnp.dot(xn, win_ref[...], preferred_element_type=F32)
    u_out[...] = proj[:, COL_U:COL_U + D_SSM]

    cqn = _rms(proj[:, COL_CQ:COL_CQ + Q_LORA], gq_ref[...]).astype(BF16)
    q = jnp.dot(cqn, wq_ref[...], preferred_element_type=F32)
    q_up = pltpu.roll(q, ROPE_HALF, 1)
    q_dn = pltpu.roll(q, D_HEADS_PAD - ROPE_HALF, 1)
    cq_t, sqa_t, sqb_t = cq_ref[...], sqa_ref[...], sqb_ref[...]
    pair_major = v_key_tile is not None

    def put_head(ref, h, tile):
        if pair_major:
            ref[h // 2, :, (h % 2) * HEAD_PAD:(h % 2 + 1) * HEAD_PAD] = tile
        else:
            ref[:, h * HEAD_PAD:(h + 1) * HEAD_PAD] = tile

    for h in range(N_HEADS):
        sl = slice(h * HEAD_PAD, (h + 1) * HEAD_PAD)
        put_head(q_out, h, (q[:, sl] * cq_t + q_up[:, sl] * sqa_t + q_dn[:, sl] * sqb_t).astype(BF16))

    lat = _rms(proj[:, COL_CKV:COL_CKV + KV_LORA], gkv_ref[...])
    lat_out[...] = lat
    kr = (proj[:, COL_KPE:COL_KPE + LANES] * ck_ref[...]
          + proj[:, COL_KPE_ROT:COL_KPE_ROT + LANES] * sk_ref[...])
    kr_out[...] = kr[:, :QK_ROPE]
    if v_key_tile is None:
        return
    kpad_out, vt_out = kv_out
    latb = lat.astype(BF16)
    k_nope = jnp.dot(latb, wk_ref[...], preferred_element_type=F32)
    kr_placed = pltpu.roll(kr, QK_NOPE, 1)
    for h in range(N_HEADS):
        put_head(kpad_out, h, (k_nope[:, h * HEAD_PAD:(h + 1) * HEAD_PAD] + kr_placed).astype(BF16))
    vt = lax.dot_general(wvt_ref[...], latb, (((1,), (1,)), ((), ())),
                         preferred_element_type=F32)
    ones = lax.broadcasted_iota(jnp.int32, vt.shape, 0) % HEAD_PAD >= V_DIM
    vt = jnp.where(ones, 1.0, vt).astype(BF16)
    for s in range(vt.shape[1] // v_key_tile):
        vt_out[s] = vt[:, s * v_key_tile:(s + 1) * v_key_tile]


def _proj(x2d, tabs, w, tm, stream_len, v_key_tile):
    n = x2d.shape[0]
    cosq, sinq_up, sinq_dn, cosk, sink = tabs
    n_tab = cosq.shape[0] // tm
    row = lambda i: (i, 0)
    tab = lambda i: (i % n_tab, 0)
    rest_shape = [
        jax.ShapeDtypeStruct((n, KV_LORA), F32),
        jax.ShapeDtypeStruct((n, QK_ROPE), F32),
        jax.ShapeDtypeStruct((n, D_SSM), F32),
    ]
    rest_specs = [pl.BlockSpec((tm, KV_LORA), row), pl.BlockSpec((tm, QK_ROPE), row), pl.BlockSpec((tm, D_SSM), row)]
    if v_key_tile is None:
        out_shape = [jax.ShapeDtypeStruct((n, D_HEADS_PAD), BF16)] + rest_shape
        out_specs = [pl.BlockSpec((tm, D_HEADS_PAD), row)] + rest_specs
    else:
        assert stream_len % tm == 0 and tm % v_key_tile == 0
        tiles_per_stream = stream_len // tm
        n_streams = n // stream_len
        pair_shape = jax.ShapeDtypeStruct((n_streams, N_HEADS // 2, stream_len, 2 * HEAD_PAD), BF16)
        pair_spec = pl.BlockSpec((None, N_HEADS // 2, tm, 2 * HEAD_PAD),
                                 lambda i: (i // tiles_per_stream, 0, i % tiles_per_stream, 0))
        out_shape = [pair_shape] + rest_shape + [
            pair_shape,
            jax.ShapeDtypeStruct((n_streams, stream_len // v_key_tile, D_HEADS_PAD, v_key_tile), BF16)]
        out_specs = [pair_spec] + rest_specs + [
            pair_spec,
            pl.BlockSpec((None, tm // v_key_tile, D_HEADS_PAD, v_key_tile),
                         lambda i: (i // tiles_per_stream, i % tiles_per_stream, 0, 0))]
    return pl.pallas_call(
        functools.partial(_proj_kernel, v_key_tile=v_key_tile),
        grid=(n // tm,),
        in_specs=[
            pl.BlockSpec((tm, D_MODEL), row),
            pl.BlockSpec((tm, LANES), tab), pl.BlockSpec((tm, LANES), tab), pl.BlockSpec((tm, LANES), tab),
            pl.BlockSpec((tm, LANES), tab), pl.BlockSpec((tm, LANES), tab),
            _const_spec((1, D_MODEL)), _const_spec((D_MODEL, D_IN_EXT)),
            _const_spec((1, Q_LORA)), _const_spec((Q_LORA, D_HEADS_PAD)),
            _const_spec((1, KV_LORA)), _const_spec((KV_LORA, D_HEADS_PAD)),
            _const_spec((D_HEADS_PAD, KV_LORA)),
        ],
        out_specs=tuple(out_specs),
        out_shape=tuple(out_shape),
        compiler_params=pltpu.CompilerParams(dimension_semantics=("parallel",),
                                             vmem_limit_bytes=VMEM_LIMIT),
        name="proj",
    )(x2d, cosq, sinq_up, sinq_dn, cosk, sink, w["g_mix"], w["w_in_ext"], w["g_q_a"], w["w_q_pad"], w["g_kv_a"],
      w["w_k_pad"], w["w_v_t"])


def _attn_prompt_kernel(q_ref, k_ref, vt_ref, o_ref, s00, s01, s10, s11, *, tq):
    i = pl.program_id(2)
    key_chunk = lax.broadcasted_iota(jnp.int32, (tq, tq), 0) // CHUNK
    query_chunk = lax.broadcasted_iota(jnp.int32, (tq, tq), 1) // CHUNK
    diag_mask = key_chunk <= query_chunk
    heads = [slice(h * HEAD_PAD, (h + 1) * HEAD_PAD) for h in range(2)]
    slots = ((s00, s01), (s10, s11))
    n_sub = ATTN_SUB
    ts = tq // n_sub

    def scores(j, slot):
        off = pl.multiple_of(j * tq, tq)
        for sl, buf in zip(heads, slots[slot]):
            buf[...] = lax.dot_general(k_ref[pl.ds(off, tq), sl], q_ref[:, sl], (((1,), (1,)), ((), ())),
                                       preferred_element_type=F32)

    def consume(j, slot, state, mask):
        for s in range(n_sub):
            rows = slice(s * ts, (s + 1) * ts)
            new_state = []
            for sl, buf, (m, acc) in zip(heads, slots[slot], state):
                st = buf[rows, :]
                if mask is not None:
                    st = jnp.where(mask[rows], st, NEG_INF)
                m_new = jnp.maximum(m, jnp.max(st, axis=0, keepdims=True))
                alpha = jnp.exp2(m - m_new)
                p = jnp.exp2(st - m_new).astype(BF16)
                acc = acc * alpha + jnp.dot(vt_ref[j, sl, rows], p, preferred_element_type=F32)
                new_state.append((m_new, acc))
            state = tuple(new_state)
        return state

    def pair(p, state):
        scores(2 * p + 1, 1)
        state = consume(2 * p, 0, state, None)
        scores(2 * p + 2, 0)
        return consume(2 * p + 1, 1, state, None)

    init = tuple((jnp.full((1, tq), NEG_INF, F32), jnp.zeros((HEAD_PAD, tq), F32)) for _ in heads)
    scores(0, 0)
    state = lax.fori_loop(0, i // 2, pair, init)

    def even_tail(state):
        return consume(i, 0, state, diag_mask)

    def odd_tail(state):
        scores(i, 1)
        state = consume(i - 1, 0, state, None)
        return consume(i, 1, state, diag_mask)

    state = lax.cond(i % 2 == 0, even_tail, odd_tail, state)
    o_t = jnp.concatenate([acc[:V_DIM] / acc[V_DIM:V_DIM + 1] for _, acc in state], axis=0)
    o_ref[...] = o_t.T


def _attn_prompt(q, kpad, vt, tq):
    b, _, t, _ = q.shape
    assert tq % CHUNK == 0 and t % tq == 0 and vt.shape == (b, t // tq, D_HEADS_PAD, tq)
    return pl.pallas_call(
        functools.partial(_attn_prompt_kernel, tq=tq),
        grid=(b, N_HEADS // 2, t // tq),
        in_specs=[pl.BlockSpec((None, None, tq, 2 * HEAD_PAD), lambda bi, hp, i: (bi, hp, i, 0)),
                  pl.BlockSpec((None, None, t, 2 * HEAD_PAD), lambda bi, hp, i: (bi, hp, 0, 0)),
                  pl.BlockSpec((None, t // tq, 2 * HEAD_PAD, tq), lambda bi, hp, i: (bi, 0, hp, 0))],
        out_specs=pl.BlockSpec((None, tq, 2 * V_DIM), lambda bi, hp, i: (bi, i, hp)),
        out_shape=jax.ShapeDtypeStruct((b, t, D_ATTN), F32),
        scratch_shapes=[pltpu.VMEM((tq, tq), F32)] * 4,
        compiler_params=pltpu.CompilerParams(
            dimension_semantics=("parallel", "parallel", "arbitrary"), vmem_limit_bytes=VMEM_LIMIT),
        name="attn_prompt",
    )(q, kpad, vt)


D_KCAT = KV_LORA + LANES


def _attn_sample_kernel(q_ref, plat_ref, pkr_ref, nlat_ref, nkr_ref, wabs_ref, wv_ref, o_ref, s0, s1, sn, *, tk):
    tq = q_ref.shape[0]
    n_past = plat_ref.shape[0] // tk
    qcat = jnp.concatenate(
        [jnp.dot(q_ref[:, h * HEAD_PAD:(h + 1) * HEAD_PAD], wabs_ref[h], preferred_element_type=F32)
         for h in range(N_HEADS)], axis=0).astype(BF16)
    tiles = [(plat_ref.at[pl.ds(j * tk, tk)], pkr_ref.at[pl.ds(j * tk, tk)], (s0, s1)[j % 2])
             for j in range(n_past)] + [(nlat_ref, nkr_ref, sn)]

    def scores(lat_ref, kr_ref, buf):
        kcat = jnp.concatenate([lat_ref[...].astype(BF16), kr_ref[...]], axis=1)
        buf[...] = lax.dot_general(qcat, kcat, (((1,), (1,)), ((), ())), preferred_element_type=F32)

    def consume(lat_ref, buf, carry):
        keys = buf.shape[1]
        ts = min(keys, tk // ATTN_SUB)
        for c in range(keys // ts):
            m, l, acc = carry
            s = buf[:, c * ts:(c + 1) * ts]
            m_new = jnp.maximum(m, jnp.max(s, axis=-1, keepdims=True))
            alpha = jnp.exp2(m - m_new)
            p = jnp.exp2(s - m_new)
            l = l * alpha + jnp.sum(p, axis=-1, keepdims=True)
            acc = acc * alpha + jnp.dot(p.astype(BF16), lat_ref[c * ts:(c + 1) * ts, :].astype(BF16),
                                        preferred_element_type=F32)
            carry = (m_new, l, acc)
        return carry

    rows = N_HEADS * tq
    carry = (jnp.full((rows, 1), NEG_INF, F32), jnp.zeros((rows, 1), F32), jnp.zeros((rows, KV_LORA), F32))
    scores(*tiles[0])
    for idx, (lat_ref, _, buf) in enumerate(tiles):
        if idx + 1 < len(tiles):
            scores(*tiles[idx + 1])
        carry = consume(lat_ref, buf, carry)
    _, l, acc = carry
    o_lat = (acc / l).astype(BF16)
    out = jnp.zeros((tq, D_ATTN), F32)
    for h in range(N_HEADS):
        out = out + jnp.dot(o_lat[h * tq:(h + 1) * tq], wv_ref[h], preferred_element_type=F32)
    o_ref[...] = out


def _attn_sample(q, past_lat, past_kr, new_lat, new_kr, w, tk):
    b, tq, _ = q.shape
    past = past_lat.shape[1]
    assert past % tk == 0
    blk = lambda bi: (bi, 0, 0)
    return pl.pallas_call(
        functools.partial(_attn_sample_kernel, tk=tk),
        grid=(b,),
        in_specs=[pl.BlockSpec((None, tq, D_HEADS_PAD), blk),
                  pl.BlockSpec((None, past, KV_LORA), blk), pl.BlockSpec((None, past, LANES), blk),
                  pl.BlockSpec((None, tq, KV_LORA), blk), pl.BlockSpec((None, tq, LANES), blk),
                  _const_spec((N_HEADS, HEAD_PAD, D_KCAT)), _const_spec((N_HEADS, KV_LORA, D_ATTN))],
        out_specs=pl.BlockSpec((None, tq, D_ATTN), blk),
        out_shape=jax.ShapeDtypeStruct((b, tq, D_ATTN), F32),
        scratch_shapes=[pltpu.VMEM((N_HEADS * tq, tk), F32), pltpu.VMEM((N_HEADS * tq, tk), F32),
                        pltpu.VMEM((N_HEADS * tq, tq), F32)],
        compiler_params=pltpu.CompilerParams(dimension_semantics=("parallel",),
                                             vmem_limit_bytes=VMEM_LIMIT),
        name="attn_sample",
    )(q, past_lat, past_kr, new_lat, new_kr, w["w_q_abs"], w["w_v_heads"])


SSM_CH = LANES
SSM_ST = SSM_CH // SSM_GROUP * SSM_STATE
N_SSM_CHUNKS = D_SSM // SSM_CH


def _ssm_kernel(u_ref, h0_ref, lre_ref, lim_ref, bc_ref, cc_ref, d_ref, wglu_ref, gssm_ref,
                y_ref, hout_ref, x_s, h_s, hc_s, *, tt):
    ti = pl.program_id(1)

    @pl.when(ti == 0)
    def _():
        hc_s[...] = h0_ref[...]

    rows = tt * SUBLANES
    u = u_ref[...].reshape(rows, D_SSM)
    ub = u.astype(BF16)
    chunks = []
    for c in range(N_SSM_CHUNKS):
        re = slice(c * SSM_ST, (c + 1) * SSM_ST)
        im = slice(N_STATE + c * SSM_ST, N_STATE + (c + 1) * SSM_ST)
        chunks.append((re, im))
        xc = jnp.dot(ub[:, c * SSM_CH:(c + 1) * SSM_CH], bc_ref[c], preferred_element_type=F32)
        x_s[:, :, re] = xc[:, :SSM_ST].reshape(tt, SUBLANES, SSM_ST)
        x_s[:, :, im] = xc[:, SSM_ST:].reshape(tt, SUBLANES, SSM_ST)

    for re, im in chunks:
        lr = jnp.broadcast_to(lre_ref[:, re], (SUBLANES, SSM_ST))
        li = jnp.broadcast_to(lim_ref[:, re], (SUBLANES, SSM_ST))

        def step(t, carry, re=re, im=im, lr=lr, li=li):
            hr, hi = carry
            nhr = lr * hr - li * hi + x_s[t, :, re]
            nhi = lr * hi + li * hr + x_s[t, :, im]
            h_s[t, :, re] = nhr
            h_s[t, :, im] = nhi
            return nhr, nhi

        hr, hi = lax.fori_loop(0, tt, step, (hc_s[:, re], hc_s[:, im]), unroll=True)
        hc_s[:, re] = hr
        hc_s[:, im] = hi

    ys = []
    for c, (re, im) in enumerate(chunks):
        hcat = jnp.concatenate([h_s[:, :, re], h_s[:, :, im]], axis=-1).reshape(rows, 2 * SSM_ST)
        ys.append(jnp.dot(hcat.astype(BF16), cc_ref[c], preferred_element_type=F32))
    y = jnp.concatenate(ys, axis=1) + d_ref[...] * u
    y = jax.nn.gelu(y, approximate=True)
    z = jnp.dot(y.astype(BF16), wglu_ref[...], preferred_element_type=F32)
    y = y * (1.0 / (1.0 + jnp.exp(-z)))
    y_ref[...] = _rms(y, gssm_ref[...]).reshape(tt, SUBLANES, D_SSM)

    @pl.when(ti == pl.num_programs(1) - 1)
    def _():
        hout_ref[...] = hc_s[...]


def _ssm(u_tb, h0, w, tt):
    t, nb, _ = u_tb.shape
    assert nb % SUBLANES == 0 and t % tt == 0
    return pl.pallas_call(
        functools.partial(_ssm_kernel, tt=tt),
        grid=(nb // SUBLANES, t // tt),
        in_specs=[pl.BlockSpec((tt, SUBLANES, D_SSM), lambda bg, ti: (ti, bg, 0)),
                  pl.BlockSpec((SUBLANES, 2 * N_STATE), lambda bg, ti: (bg, 0)),
                  _const_spec((1, N_STATE)), _const_spec((1, N_STATE)),
                  _const_spec((N_SSM_CHUNKS, SSM_CH, 2 * SSM_ST)), _const_spec((N_SSM_CHUNKS, 2 * SSM_ST, SSM_CH)),
                  _const_spec((1, D_SSM)), _const_spec((D_SSM, D_SSM)), _const_spec((1, D_SSM))],
        out_specs=(pl.BlockSpec((tt, SUBLANES, D_SSM), lambda bg, ti: (ti, bg, 0)),
                   pl.BlockSpec((SUBLANES, 2 * N_STATE), lambda bg, ti: (bg, 0))),
        out_shape=(jax.ShapeDtypeStruct((t, nb, D_SSM), F32),
                   jax.ShapeDtypeStruct((nb, 2 * N_STATE), F32)),
        scratch_shapes=[pltpu.VMEM((tt, SUBLANES, 2 * N_STATE), F32),
                        pltpu.VMEM((tt, SUBLANES, 2 * N_STATE), F32),
                        pltpu.VMEM((SUBLANES, 2 * N_STATE), F32)],
        compiler_params=pltpu.CompilerParams(dimension_semantics=("parallel", "arbitrary"),
                                             vmem_limit_bytes=VMEM_LIMIT),
        name="ssm",
    )(u_tb, h0, w["lam_re"], w["lam_im"], w["b_chunks"], w["c_chunks"], w["d_skip"], w["w_glu"],
      w["g_ssm_out"])


def _post_kernel(x_ref, attn_ref, ssm_ref, gattn_ref, wouta_ref, wouts_ref, gmlp_ref, wup_ref, wdown_ref,
                 gfin_ref, y_ref):
    an = _rms(attn_ref[...], gattn_ref[...]).astype(BF16)
    mixed = (jnp.dot(an, wouta_ref[...], preferred_element_type=F32)
             + jnp.dot(ssm_ref[...].astype(BF16), wouts_ref[...], preferred_element_type=F32))
    h = x_ref[...] + mixed
    hn = _rms(h, gmlp_ref[...]).astype(BF16)
    acc = jnp.zeros(h.shape, F32)
    for c in range(D_FF // FF_CHUNK):
        ff = slice(c * FF_CHUNK, (c + 1) * FF_CHUNK)
        a = jnp.dot(hn, wup_ref[:, ff], preferred_element_type=F32)
        a = jnp.square(jnp.maximum(a, 0.0))
        acc = acc + jnp.dot(a.astype(BF16), wdown_ref[ff, :], preferred_element_type=F32)
    y_ref[...] = _rms(h + acc, gfin_ref[...])


def _post(x2d, attn2d, ssm2d, w, tm):
    n = x2d.shape[0]
    row = lambda i: (i, 0)
    return pl.pallas_call(
        _post_kernel,
        grid=(n // tm,),
        in_specs=[pl.BlockSpec((tm, D_MODEL), row), pl.BlockSpec((tm, D_ATTN), row),
                  pl.BlockSpec((tm, D_SSM), row),
                  _const_spec((1, D_ATTN)), _const_spec((D_ATTN, D_MODEL)), _const_spec((D_SSM, D_MODEL)),
                  _const_spec((1, D_MODEL)), _const_spec((D_MODEL, D_FF)), _const_spec((D_FF, D_MODEL)),
                  _const_spec((1, D_MODEL))],
        out_specs=pl.BlockSpec((tm, D_MODEL), row),
        out_shape=jax.ShapeDtypeStruct((n, D_MODEL), F32),
        compiler_params=pltpu.CompilerParams(dimension_semantics=("parallel",),
                                             vmem_limit_bytes=VMEM_LIMIT),
        name="post",
    )(x2d, attn2d, ssm2d, w["g_attn_out"], w["w_out_attn"], w["w_out_ssm"], w["g_mlp"], w["w_up"],
      w["w_down"], w["g_final"])


def _rope_tables(pos):
    t = pos.shape[0]
    inv_freq = ROPE_THETA ** (-(jnp.arange(ROPE_HALF, dtype=F32) * 2.0) / QK_ROPE)
    ang = pos.astype(F32)[:, None] * inv_freq[None, :]
    cc = jnp.tile(jnp.cos(ang), (1, 2))
    ss = jnp.tile(jnp.sin(ang), (1, 2))
    pad_hi = jnp.zeros((t, HEAD_PAD - QK_NOPE - QK_ROPE), F32)
    qs = SOFTMAX_SCALE * LOG2_E
    cosq = jnp.concatenate([jnp.full((t, QK_NOPE), qs, F32), cc * qs, pad_hi], axis=1)
    z_half = jnp.zeros((t, ROPE_HALF), F32)
    sin_h = jnp.sin(ang) * qs
    sinq_up = jnp.concatenate([jnp.zeros((t, QK_NOPE), F32), z_half, sin_h, pad_hi], axis=1)
    sinq_dn = jnp.concatenate([jnp.zeros((t, QK_NOPE), F32), -sin_h, z_half, pad_hi], axis=1)
    pad_k = jnp.zeros((t, LANES - QK_ROPE), F32)
    cosk = jnp.concatenate([cc, pad_k], axis=1)
    sink = jnp.concatenate([ss, pad_k], axis=1)
    return cosq, sinq_up, sinq_dn, cosk, sink


def _rot_cols(w_x1, w_x2):
    return -w_x2, w_x1


def _layer_weights(g_mix, w_in, g_q_a, w_q_up, g_kv_a, w_kv_up, a_re, a_im, log_step, b_re, b_im,
                   c_re, c_im, d_skip, w_glu, g_attn_out, g_ssm_out, w_out, g_mlp, w_up, w_down, g_final):
    w = {}
    w["g_mix"] = g_mix[None, :]
    w_cq, w_ckv = w_in[:, :Q_LORA], w_in[:, Q_LORA:Q_LORA + KV_LORA]
    w_kpe = w_in[:, Q_LORA + KV_LORA:Q_LORA + KV_LORA + QK_ROPE]
    w_u = w_in[:, Q_LORA + KV_LORA + QK_ROPE:]
    rot1, rot2 = _rot_cols(w_kpe[:, :ROPE_HALF], w_kpe[:, ROPE_HALF:])
    zk = jnp.zeros((D_MODEL, LANES - QK_ROPE), F32)
    w["w_in_ext"] = jnp.concatenate([w_cq, w_ckv, w_u, w_kpe, zk, rot1, rot2, zk], axis=1).astype(BF16)

    w["g_q_a"] = g_q_a[None, :]
    wq = w_q_up.reshape(Q_LORA, N_HEADS, QK_NOPE + QK_ROPE)
    nope, r1, r2 = wq[:, :, :QK_NOPE], wq[:, :, QK_NOPE:QK_NOPE + ROPE_HALF], wq[:, :, QK_NOPE + ROPE_HALF:]
    zq = jnp.zeros((Q_LORA, N_HEADS, HEAD_PAD - QK_NOPE - QK_ROPE), F32)
    w["w_q_pad"] = jnp.concatenate([nope, r1, r2, zq], axis=2).reshape(Q_LORA, D_HEADS_PAD).astype(BF16)

    w["g_kv_a"] = g_kv_a[None, :]
    wkv = w_kv_up.reshape(KV_LORA, N_HEADS, QK_NOPE + V_DIM)
    zkv = jnp.zeros((KV_LORA, N_HEADS, HEAD_PAD - QK_NOPE), F32)
    wk_pad = jnp.concatenate([wkv[:, :, :QK_NOPE], zkv], axis=2).reshape(KV_LORA, D_HEADS_PAD)
    wv_pad = jnp.concatenate([wkv[:, :, QK_NOPE:], zkv], axis=2).reshape(KV_LORA, D_HEADS_PAD)
    w["w_k_pad"] = wk_pad.astype(BF16)
    w["w_v_t"] = wv_pad.T.astype(BF16)
    wk_t = jnp.transpose(wkv[:, :, :QK_NOPE], (1, 2, 0))
    q_abs = jnp.zeros((N_HEADS, HEAD_PAD, D_KCAT), F32).at[:, :QK_NOPE, :KV_LORA].set(wk_t)
    q_abs = q_abs.at[:, QK_NOPE + jnp.arange(QK_ROPE), KV_LORA + jnp.arange(QK_ROPE)].set(1.0)
    w["w_q_abs"] = q_abs.astype(BF16)
    wv_h = jnp.transpose(wkv[:, :, QK_NOPE:], (1, 0, 2))
    w["w_v_heads"] = jnp.einsum("hce,hg->hcge", wv_h, jnp.eye(N_HEADS, dtype=F32)).reshape(
        N_HEADS, KV_LORA, D_ATTN).astype(BF16)

    lam_re, lam_im, bb_re, bb_im = _ssm_prep(a_re, a_im, log_step, b_re, b_im)
    w["lam_re"] = lam_re.reshape(1, N_STATE)
    w["lam_im"] = lam_im.reshape(1, N_STATE)
    gpc = SSM_CH // SSM_GROUP
    eye = jnp.eye(gpc, dtype=F32)
    by_chunk = lambda m: m.reshape(N_SSM_CHUNKS, gpc, SSM_GROUP, SSM_STATE)
    blk = lambda m: jnp.einsum("cgpn,gh->cgphn", by_chunk(m), eye).reshape(N_SSM_CHUNKS, SSM_CH, SSM_ST)
    w["b_chunks"] = jnp.concatenate([blk(bb_re), blk(bb_im)], axis=2).astype(BF16)
    blk_t = lambda m: jnp.einsum("cgpn,gh->cgnhp", by_chunk(m), eye).reshape(N_SSM_CHUNKS, SSM_ST, SSM_CH)
    w["c_chunks"] = jnp.concatenate([blk_t(c_re), blk_t(-c_im)], axis=1).astype(BF16)
    w["d_skip"] = d_skip[None, :]
    w["w_glu"] = w_glu.astype(BF16)
    w["g_ssm_out"] = g_ssm_out[None, :]

    w["g_attn_out"] = g_attn_out[None, :]
    w["w_out_attn"] = w_out[:D_ATTN].astype(BF16)
    w["w_out_ssm"] = w_out[D_ATTN:].astype(BF16)
    w["g_mlp"] = g_mlp[None, :]
    w["w_up"] = w_up.astype(BF16)
    w["w_down"] = w_down.astype(BF16)
    w["g_final"] = g_final[None, :]
    return w


def _pack_state(h_re, h_im):
    nb = h_re.shape[0]
    return jnp.concatenate([h_re.reshape(nb, N_STATE), h_im.reshape(nb, N_STATE)], axis=1)


def _unpack_state(h):
    nb = h.shape[0]
    return (h[:, :N_STATE].reshape(nb, N_GROUPS, SSM_STATE), h[:, N_STATE:].reshape(nb, N_GROUPS, SSM_STATE))


def _branch(x, pos, past, h0, w, *, proj_tm, post_tm, ssm_tt):
    b, t, _ = x.shape
    n = b * t
    x2d = x.reshape(n, D_MODEL)
    tabs = _rope_tables(pos)
    if t < proj_tm:
        assert proj_tm % t == 0
        tabs = tuple(jnp.tile(a, (proj_tm // t, 1)) for a in tabs)
    else:
        assert t % proj_tm == 0
    if past is None:
        tq = min(ATTN_TQ, t)
        q, lat, kr, u, kpad, vt = _proj(x2d, tabs, w, proj_tm, t, tq)
        attn = _attn_prompt(q, kpad, vt, tq)
    else:
        q, lat, kr, u = _proj(x2d, tabs, w, proj_tm, t, None)
        past_lat, past_kr = past
        plen = past_lat.shape[1]
        assert plen % CHUNK == 0 and t <= CHUNK
        lane_pad = lambda a: jnp.pad(a, ((0, 0), (0, 0), (0, LANES - QK_ROPE))).astype(BF16)
        attn = _attn_sample(q.reshape(b, t, D_HEADS_PAD), past_lat, lane_pad(past_kr),
                            lat.reshape(b, t, KV_LORA), lane_pad(kr.reshape(b, t, QK_ROPE)), w,
                            min(ATTN_TK_SAMPLE, plen))
    u_tb = jnp.swapaxes(u.reshape(b, t, D_SSM), 0, 1)
    y_tb, h_fin = _ssm(u_tb, h0, w, min(ssm_tt, t))
    ssm2d = jnp.swapaxes(y_tb, 0, 1).reshape(n, D_SSM)
    y = _post(x2d, attn.reshape(n, D_ATTN), ssm2d, w, post_tm)
    h_re, h_im = _unpack_state(h_fin)
    return (y.reshape(b, t, D_MODEL), lat.reshape(1, b, t, KV_LORA), kr.reshape(1, b, t, QK_ROPE),
            h_re[None], h_im[None])


def kernel(x_prompt, x_sample, cache_kv_latent, cache_k_rope, state_ssm_re, state_ssm_im, g_mix, w_in, g_q_a,
           w_q_up, g_kv_a, w_kv_up, a_re, a_im, log_step, b_re, b_im, c_re, c_im, d_skip, w_glu, g_attn_out,
           g_ssm_out, w_out, g_mlp, w_up, w_down, g_final):
    assert g_mix.shape[0] == 1, "single-layer trunk"
    w = _layer_weights(g_mix[0], w_in[0], g_q_a[0], w_q_up[0], g_kv_a[0], w_kv_up[0], a_re[0], a_im[0],
                       log_step[0], b_re[0], b_im[0], c_re[0], c_im[0], d_skip[0], w_glu[0], g_attn_out[0],
                       g_ssm_out[0], w_out[0], g_mlp[0], w_up[0], w_down[0], g_final)
    tiles = dict(proj_tm=PROJ_TM, post_tm=POST_TM, ssm_tt=SSM_TT)

    bp, tp, _ = x_prompt.shape
    pos_p = jnp.arange(tp, dtype=jnp.int32)
    h0p = jnp.zeros((bp, 2 * N_STATE), F32)
    y_p, lat_p, kr_p, hr_p, hi_p = _branch(x_prompt, pos_p, None, h0p, w, **tiles)

    bs, ts, _ = x_sample.shape
    plen = cache_kv_latent.shape[2]
    pos_s = plen + jnp.arange(ts, dtype=jnp.int32)
    h0s = _pack_state(state_ssm_re[0], state_ssm_im[0])
    y_s, lat_s, kr_s, hr_s, hi_s = _branch(x_sample, pos_s, (cache_kv_latent[0], cache_k_rope[0]), h0s, w, **tiles)
    return (y_p, y_s, lat_p, kr_p, hr_p, hi_p, lat_s, kr_s, hr_s, hi_s)
```

```python
import functools
import math

import jax
import jax.numpy as jnp
from jax import lax
from jax.experimental import pallas as pl
from jax.experimental.pallas import tpu as pltpu

F32 = jnp.float32
BF16 = jnp.bfloat16

D_MODEL = 1024
N_HEADS = 8
QK_NOPE = 64
QK_ROPE = 32
ROPE_HALF = QK_ROPE // 2
V_DIM = 64
KV_LORA = 256
Q_LORA = 768
D_ATTN = N_HEADS * V_DIM
D_SSM = 512
SSM_GROUP = 16
N_GROUPS = D_SSM // SSM_GROUP
SSM_STATE = 64
N_STATE = N_GROUPS * SSM_STATE
D_FF = 4 * D_MODEL
CHUNK = 64
ROPE_THETA = 10000.0
SOFTMAX_SCALE = (QK_NOPE + QK_ROPE) ** -0.5
LOG2_E = math.log2(math.e)
EPS = 1e-6
NEG_INF = -1e30

LANES = 128
SUBLANES = 8
HEAD_PAD = LANES
D_HEADS_PAD = N_HEADS * HEAD_PAD
COL_CQ = 0
COL_CKV = Q_LORA
COL_U = Q_LORA + KV_LORA
COL_KPE = COL_U + D_SSM
COL_KPE_ROT = COL_KPE + LANES
D_IN_EXT = COL_KPE_ROT + LANES

VMEM_LIMIT = 56 * 1024 * 1024

PROJ_TM = 512
ATTN_TQ = 512
ATTN_TK_SAMPLE = 512
ATTN_SUB = 2
SSM_TT = 64
POST_TM = 512
FF_CHUNK = 1024


def _const_spec(shape):
    nd = len(shape)
    return pl.BlockSpec(shape, lambda *_: (0,) * nd, pipeline_mode=pl.Buffered(1))


def _rms(x, g):
    return x * lax.rsqrt(jnp.mean(x * x, axis=-1, keepdims=True) + EPS) * g


def _prep_kernel(are_ref, aim_ref, ls_ref, bre_ref, bim_ref, lre_ref, lim_ref, bbre_ref, bbim_ref):
    dt = jnp.exp(ls_ref[...])
    lr, li = are_ref[...], aim_ref[...]
    mag = jnp.exp(lr * dt)
    lb_re, lb_im = mag * jnp.cos(li * dt), mag * jnp.sin(li * dt)
    nr, ni = lb_re - 1.0, lb_im
    den = lr * lr + li * li
    coef_re = (nr * lr + ni * li) / den
    coef_im = (ni * lr - nr * li) / den
    lre_ref[...] = lb_re
    lim_ref[...] = lb_im
    bre, bim = bre_ref[...], bim_ref[...]
    cr, ci = coef_re[:, None, :], coef_im[:, None, :]
    bbre_ref[...] = cr * bre - ci * bim
    bbim_ref[...] = cr * bim + ci * bre


def _ssm_prep(a_re, a_im, log_step, b_re, b_im):
    g, n = a_re.shape
    p = b_re.shape[-1]
    bre_t = jnp.swapaxes(b_re, 1, 2)
    bim_t = jnp.swapaxes(b_im, 1, 2)
    return pl.pallas_call(
        _prep_kernel,
        out_shape=(jax.ShapeDtypeStruct((g, n), F32), jax.ShapeDtypeStruct((g, n), F32),
                   jax.ShapeDtypeStruct((g, p, n), F32), jax.ShapeDtypeStruct((g, p, n), F32)),
        name="ssm_prep",
    )(a_re, a_im, log_step.reshape(g, 1), bre_t, bim_t)


def _proj_kernel(x_ref, cq_ref, sqa_ref, sqb_ref, ck_ref, sk_ref, gmix_ref, win_ref, gq_ref, wq_ref, gkv_ref,
                 wk_ref, wvt_ref,
                 q_out, lat_out, kr_out, u_out, *kv_out, v_key_tile):
    xn = _rms(x_ref[...], gmix_ref[...]).astype(BF16)
    proj = jnp.dot(xn, win_ref[...], preferred_element_type=F32)
    u_out[...] = proj[:, COL_U:COL_U + D_SSM]

    cqn = _rms(proj[:, COL_CQ:COL_CQ + Q_LORA], gq_ref[...]).astype(BF16)
    q = jnp.dot(cqn, wq_ref[...], preferred_element_type=F32)
    q_up = pltpu.roll(q, ROPE_HALF, 1)
    q_dn = pltpu.roll(q, D_HEADS_PAD - ROPE_HALF, 1)
    cq_t, sqa_t, sqb_t = cq_ref[...], sqa_ref[...], sqb_ref[...]
    pair_major = v_key_tile is not None

    def put_head(ref, h, tile):
        if pair_major:
            ref[h // 2, :, (h % 2) * HEAD_PAD:(h % 2 + 1) * HEAD_PAD] = tile
        else:
            ref[:, h * HEAD_PAD:(h + 1) * HEAD_PAD] = tile

    for h in range(N_HEADS):
        sl = slice(h * HEAD_PAD, (h + 1) * HEAD_PAD)
        put_head(q_out, h, (q[:, sl] * cq_t + q_up[:, sl] * sqa_t + q_dn[:, sl] * sqb_t).astype(BF16))

    lat = _rms(proj[:, COL_CKV:COL_CKV + KV_LORA], gkv_ref[...])
    lat_out[...] = lat
    kr = (proj[:, COL_KPE:COL_KPE + LANES] * ck_ref[...]
          + proj[:, COL_KPE_ROT:COL_KPE_ROT + LANES] * sk_ref[...])
    kr_out[...] = kr[:, :QK_ROPE]
    if v_key_tile is None:
        return
    kpad_out, vt_out = kv_out
    latb = lat.astype(BF16)
    k_nope = jnp.dot(latb, wk_ref[...], preferred_element_type=F32)
    kr_placed = pltpu.roll(kr, QK_NOPE, 1)
    for h in range(N_HEADS):
        put_head(kpad_out, h, (k_nope[:, h * HEAD_PAD:(h + 1) * HEAD_PAD] + kr_placed).astype(BF16))
    vt = lax.dot_general(wvt_ref[...], latb, (((1,), (1,)), ((), ())),
                         preferred_element_type=F32)
    ones = lax.broadcasted_iota(jnp.int32, vt.shape, 0) % HEAD_PAD >= V_DIM
    vt = jnp.where(ones, 1.0, vt).astype(BF16)
    for s in range(vt.shape[1] // v_key_tile):
        vt_out[s] = vt[:, s * v_key_tile:(s + 1) * v_key_tile]


def _proj(x2d, tabs, w, tm, stream_len, v_key_tile):
    n = x2d.shape[0]
    cosq, sinq_up, sinq_dn, cosk, sink = tabs
    n_tab = cosq.shape[0] // tm
    row = lambda i: (i, 0)
    tab = lambda i: (i % n_tab, 0)
    rest_shape = [
        jax.ShapeDtypeStruct((n, KV_LORA), F32),
        jax.ShapeDtypeStruct((n, QK_ROPE), F32),
        jax.ShapeDtypeStruct((n, D_SSM), F32),
    ]
    rest_specs = [pl.BlockSpec((tm, KV_LORA), row), pl.BlockSpec((tm, QK_ROPE), row), pl.BlockSpec((tm, D_SSM), row)]
    if v_key_tile is None:
        out_shape = [jax.ShapeDtypeStruct((n, D_HEADS_PAD), BF16)] + rest_shape
        out_specs = [pl.BlockSpec((tm, D_HEADS_PAD), row)] + rest_specs
    else:
        assert stream_len % tm == 0 and tm % v_key_tile == 0
        tiles_per_stream = stream_len // tm
        n_streams = n // stream_len
        pair_shape = jax.ShapeDtypeStruct((n_streams, N_HEADS // 2, stream_len, 2 * HEAD_PAD), BF16)
        pair_spec = pl.BlockSpec((None, N_HEADS // 2, tm, 2 * HEAD_PAD),
                                 lambda i: (i // tiles_per_stream, 0, i % tiles_per_stream, 0))
        out_shape = [pair_shape] + rest_shape + [
            pair_shape,
            jax.ShapeDtypeStruct((n_streams, stream_len // v_key_tile, D_HEADS_PAD, v_key_tile), BF16)]
        out_specs = [pair_spec] + rest_specs + [
            pair_spec,
            pl.BlockSpec((None, tm // v_key_tile, D_HEADS_PAD, v_key_tile),
                         lambda i: (i // tiles_per_stream, i % tiles_per_stream, 0, 0))]
    return pl.pallas_call(
        functools.partial(_proj_kernel, v_key_tile=v_key_tile),
        grid=(n // tm,),
        in_specs=[
            pl.BlockSpec((tm, D_MODEL), row),
            pl.BlockSpec((tm, LANES), tab), pl.BlockSpec((tm, LANES), tab), pl.BlockSpec((tm, LANES), tab),
            pl.BlockSpec((tm, LANES), tab), pl.BlockSpec((tm, LANES), tab),
            _const_spec((1, D_MODEL)), _const_spec((D_MODEL, D_IN_EXT)),
            _const_spec((1, Q_LORA)), _const_spec((Q_LORA, D_HEADS_PAD)),
            _const_spec((1, KV_LORA)), _const_spec((KV_LORA, D_HEADS_PAD)),
            _const_spec((D_HEADS_PAD, KV_LORA)),
        ],
        out_specs=tuple(out_specs),
        out_shape=tuple(out_shape),
        compiler_params=pltpu.CompilerParams(dimension_semantics=("parallel",),
                                             vmem_limit_bytes=VMEM_LIMIT),
        name="proj",
    )(x2d, cosq, sinq_up, sinq_dn, cosk, sink, w["g_mix"], w["w_in_ext"], w["g_q_a"], w["w_q_pad"], w["g_kv_a"],
      w["w_k_pad"], w["w_v_t"])


def _attn_prompt_kernel(q_ref, k_ref, vt_ref, o_ref, s00, s01, s10, s11, *, tq):
    i = pl.program_id(2)
    key_chunk = lax.broadcasted_iota(jnp.int32, (tq, tq), 0) // CHUNK
    query_chunk = lax.broadcasted_iota(jnp.int32, (tq, tq), 1) // CHUNK
    diag_mask = key_chunk <= query_chunk
    heads = [slice(h * HEAD_PAD, (h + 1) * HEAD_PAD) for h in range(2)]
    slots = ((s00, s01), (s10, s11))
    n_sub = ATTN_SUB
    ts = tq // n_sub

    def scores(j, slot):
        off = pl.multiple_of(j * tq, tq)
        for sl, buf in zip(heads, slots[slot]):
            buf[...] = lax.dot_general(k_ref[pl.ds(off, tq), sl], q_ref[:, sl], (((1,), (1,)), ((), ())),
                                       preferred_element_type=F32)

    def consume(j, slot, state, mask):
        for s in range(n_sub):
            rows = slice(s * ts, (s + 1) * ts)
            new_state = []
            for sl, buf, (m, acc) in zip(heads, slots[slot], state):
                st = buf[rows, :]
                if mask is not None:
                    st = jnp.where(mask[rows], st, NEG_INF)
                m_new = jnp.maximum(m, jnp.max(st, axis=0, keepdims=True))
                alpha = jnp.exp2(m - m_new)
                p = jnp.exp2(st - m_new).astype(BF16)
                acc = acc * alpha + jnp.dot(vt_ref[j, sl, rows], p, preferred_element_type=F32)
                new_state.append((m_new, acc))
            state = tuple(new_state)
        return state

    def pair(p, state):
        scores(2 * p + 1, 1)
        state = consume(2 * p, 0, state, None)
        scores(2 * p + 2, 0)
        return consume(2 * p + 1, 1, state, None)

    init = tuple((jnp.full((1, tq), NEG_INF, F32), jnp.zeros((HEAD_PAD, tq), F32)) for _ in heads)
    scores(0, 0)
    state = lax.fori_loop(0, i // 2, pair, init)

    def even_tail(state):
        return consume(i, 0, state, diag_mask)

    def odd_tail(state):
        scores(i, 1)
        state = consume(i - 1, 0, state, None)
        return consume(i, 1, state, diag_mask)

    state = lax.cond(i % 2 == 0, even_tail, odd_tail, state)
    o_t = jnp.concatenate([acc[:V_DIM] / acc[V_DIM:V_DIM + 1] for _, acc in state], axis=0)
    o_ref[...] = o_t.T


def _attn_prompt(q, kpad, vt, tq):
    b, _, t, _ = q.shape
    assert tq % CHUNK == 0 and t % tq == 0 and vt.shape == (b, t // tq, D_HEADS_PAD, tq)
    return pl.pallas_call(
        functools.partial(_attn_prompt_kernel, tq=tq),
        grid=(b, N_HEADS // 2, t // tq),
        in_specs=[pl.BlockSpec((None, None, tq, 2 * HEAD_PAD), lambda bi, hp, i: (bi, hp, i, 0)),
                  pl.BlockSpec((None, None, t, 2 * HEAD_PAD), lambda bi, hp, i: (bi, hp, 0, 0)),
                  pl.BlockSpec((None, t // tq, 2 * HEAD_PAD, tq), lambda bi, hp, i: (bi, 0, hp, 0))],
        out_specs=pl.BlockSpec((None, tq, 2 * V_DIM), lambda bi, hp, i: (bi, i, hp)),
        out_shape=jax.ShapeDtypeStruct((b, t, D_ATTN), F32),
        scratch_shapes=[pltpu.VMEM((tq, tq), F32)] * 4,
        compiler_params=pltpu.CompilerParams(
            dimension_semantics=("parallel", "parallel", "arbitrary"), vmem_limit_bytes=VMEM_LIMIT),
        name="attn_prompt",
    )(q, kpad, vt)


D_KCAT = KV_LORA + LANES


def _attn_sample_kernel(q_ref, plat_ref, pkr_ref, nlat_ref, nkr_ref, wabs_ref, wv_ref, o_ref, s0, s1, sn, *, tk):
    tq = q_ref.shape[0]
    n_past = plat_ref.shape[0] // tk
    qcat = jnp.concatenate(
        [jnp.dot(q_ref[:, h * HEAD_PAD:(h + 1) * HEAD_PAD], wabs_ref[h], preferred_element_type=F32)
         for h in range(N_HEADS)], axis=0).astype(BF16)
    tiles = [(plat_ref.at[pl.ds(j * tk, tk)], pkr_ref.at[pl.ds(j * tk, tk)], (s0, s1)[j % 2])
             for j in range(n_past)] + [(nlat_ref, nkr_ref, sn)]

    def scores(lat_ref, kr_ref, buf):
        kcat = jnp.concatenate([lat_ref[...].astype(BF16), kr_ref[...]], axis=1)
        buf[...] = lax.dot_general(qcat, kcat, (((1,), (1,)), ((), ())), preferred_element_type=F32)

    def consume(lat_ref, buf, carry):
        keys = buf.shape[1]
        ts = min(keys, tk // ATTN_SUB)
        for c in range(keys // ts):
            m, l, acc = carry
            s = buf[:, c * ts:(c + 1) * ts]
            m_new = jnp.maximum(m, jnp.max(s, axis=-1, keepdims=True))
            alpha = jnp.exp2(m - m_new)
            p = jnp.exp2(s - m_new)
            l = l * alpha + jnp.sum(p, axis=-1, keepdims=True)
            acc = acc * alpha + jnp.dot(p.astype(BF16), lat_ref[c * ts:(c + 1) * ts, :].astype(BF16),
                                        preferred_element_type=F32)
            carry = (m_new, l, acc)
        return carry

    rows = N_HEADS * tq
    carry = (jnp.full((rows, 1), NEG_INF, F32), jnp.zeros((rows, 1), F32), jnp.zeros((rows, KV_LORA), F32))
    scores(*tiles[0])
    for idx, (lat_ref, _, buf) in enumerate(tiles):
        if idx + 1 < len(tiles):
            scores(*tiles[idx + 1])
        carry = consume(lat_ref, buf, carry)
    _, l, acc = carry
    o_lat = (acc / l).astype(BF16)
    out = jnp.zeros((tq, D_ATTN), F32)
    for h in range(N_HEADS):
        out = out + jnp.dot(o_lat[h * tq:(h + 1) * tq], wv_ref[h], preferred_element_type=F32)
    o_ref[...] = out


def _attn_sample(q, past_lat, past_kr, new_lat, new_kr, w, tk):
    b, tq, _ = q.shape
    past = past_lat.shape[1]
    assert past % tk == 0
    blk = lambda bi: (bi, 0, 0)
    return pl.pallas_call(
        functools.partial(_attn_sample_kernel, tk=tk),
        grid=(b,),
        in_specs=[pl.BlockSpec((None, tq, D_HEADS_PAD), blk),
                  pl.BlockSpec((None, past, KV_LORA), blk), pl.BlockSpec((None, past, LANES), blk),
                  pl.BlockSpec((None, tq, KV_LORA), blk), pl.BlockSpec((None, tq, LANES), blk),
                  _const_spec((N_HEADS, HEAD_PAD, D_KCAT)), _const_spec((N_HEADS, KV_LORA, D_ATTN))],
        out_specs=pl.BlockSpec((None, tq, D_ATTN), blk),
        out_shape=jax.ShapeDtypeStruct((b, tq, D_ATTN), F32),
        scratch_shapes=[pltpu.VMEM((N_HEADS * tq, tk), F32), pltpu.VMEM((N_HEADS * tq, tk), F32),
                        pltpu.VMEM((N_HEADS * tq, tq), F32)],
        compiler_params=pltpu.CompilerParams(dimension_semantics=("parallel",),
                                             vmem_limit_bytes=VMEM_LIMIT),
        name="attn_sample",
    )(q, past_lat, past_kr, new_lat, new_kr, w["w_q_abs"], w["w_v_heads"])


SSM_CH = LANES
SSM_ST = SSM_CH // SSM_GROUP * SSM_STATE
N_SSM_CHUNKS = D_SSM // SSM_CH


def _ssm_kernel(u_ref, h0_ref, lre_ref, lim_ref, bc_ref, cc_ref, d_ref, wglu_ref, gssm_ref,
                y_ref, hout_ref, x_s, h_s, hc_s, *, tt):
    ti = pl.program_id(1)

    @pl.when(ti == 0)
    def _():
        hc_s[...] = h0_ref[...]

    rows = tt * SUBLANES
    u = pltpu.einshape("btd->tbd", u_ref[...]).reshape(rows, D_SSM)
    ub = u.astype(BF16)
    chunks = []
    for c in range(N_SSM_CHUNKS):
        re = slice(c * SSM_ST, (c + 1) * SSM_ST)
        im = slice(N_STATE + c * SSM_ST, N_STATE + (c + 1) * SSM_ST)
        chunks.append((re, im))
        xc = jnp.dot(ub[:, c * SSM_CH:(c + 1) * SSM_CH], bc_ref[c], preferred_element_type=F32)
        x_s[:, :, re] = xc[:, :SSM_ST].reshape(tt, SUBLANES, SSM_ST)
        x_s[:, :, im] = xc[:, SSM_ST:].reshape(tt, SUBLANES, SSM_ST)

    for re, im in chunks:
        lr = jnp.broadcast_to(lre_ref[:, re], (SUBLANES, SSM_ST))
        li = jnp.broadcast_to(lim_ref[:, re], (SUBLANES, SSM_ST))

        def step(t, carry, re=re, im=im, lr=lr, li=li):
            hr, hi = carry
            nhr = lr * hr - li * hi + x_s[t, :, re]
            nhi = lr * hi + li * hr + x_s[t, :, im]
            h_s[t, :, re] = nhr
            h_s[t, :, im] = nhi
            return nhr, nhi

        hr, hi = lax.fori_loop(0, tt, step, (hc_s[:, re], hc_s[:, im]), unroll=True)
        hc_s[:, re] = hr
        hc_s[:, im] = hi

    ys = []
    for c, (re, im) in enumerate(chunks):
        hcat = jnp.concatenate([h_s[:, :, re], h_s[:, :, im]], axis=-1).reshape(rows, 2 * SSM_ST)
        ys.append(jnp.dot(hcat.astype(BF16), cc_ref[c], preferred_element_type=F32))
    y = jnp.concatenate(ys, axis=1) + d_ref[...] * u
    y = jax.nn.gelu(y, approximate=True)
    z = jnp.dot(y.astype(BF16), wglu_ref[...], preferred_element_type=F32)
    y = y * (1.0 / (1.0 + jnp.exp(-z)))
    y_ref[...] = pltpu.einshape("tbd->btd", _rms(y, gssm_ref[...]).reshape(tt, SUBLANES, D_SSM))

    @pl.when(ti == pl.num_programs(1) - 1)
    def _():
        hout_ref[...] = hc_s[...]


def _ssm(u, h0, w, tt):
    nb, t, _ = u.shape
    assert nb % SUBLANES == 0 and t % tt == 0
    return pl.pallas_call(
        functools.partial(_ssm_kernel, tt=tt),
        grid=(nb // SUBLANES, t // tt),
        in_specs=[pl.BlockSpec((SUBLANES, tt, D_SSM), lambda bg, ti: (bg, ti, 0)),
                  pl.BlockSpec((SUBLANES, 2 * N_STATE), lambda bg, ti: (bg, 0)),
                  _const_spec((1, N_STATE)), _const_spec((1, N_STATE)),
                  _const_spec((N_SSM_CHUNKS, SSM_CH, 2 * SSM_ST)), _const_spec((N_SSM_CHUNKS, 2 * SSM_ST, SSM_CH)),
                  _const_spec((1, D_SSM)), _const_spec((D_SSM, D_SSM)), _const_spec((1, D_SSM))],
        out_specs=(pl.BlockSpec((SUBLANES, tt, D_SSM), lambda bg, ti: (bg, ti, 0)),
                   pl.BlockSpec((SUBLANES, 2 * N_STATE), lambda bg, ti: (bg, 0))),
        out_shape=(jax.ShapeDtypeStruct((nb, t, D_SSM), F32),
                   jax.ShapeDtypeStruct((nb, 2 * N_STATE), F32)),
        scratch_shapes=[pltpu.VMEM((tt, SUBLANES, 2 * N_STATE), F32),
                        pltpu.VMEM((tt, SUBLANES, 2 * N_STATE), F32),
                        pltpu.VMEM((SUBLANES, 2 * N_STATE), F32)],
        compiler_params=pltpu.CompilerParams(dimension_semantics=("parallel", "arbitrary"),
                                             vmem_limit_bytes=VMEM_LIMIT),
        name="ssm",
    )(u, h0, w["lam_re"], w["lam_im"], w["b_chunks"], w["c_chunks"], w["d_skip"], w["w_glu"],
      w["g_ssm_out"])


def _post_kernel(x_ref, attn_ref, ssm_ref, gattn_ref, wouta_ref, wouts_ref, gmlp_ref, wup_ref, wdown_ref,
                 gfin_ref, y_ref):
    an = _rms(attn_ref[...], gattn_ref[...]).astype(BF16)
    mixed = (jnp.dot(an, wouta_ref[...], preferred_element_type=F32)
             + jnp.dot(ssm_ref[...].astype(BF16), wouts_ref[...], preferred_element_type=F32))
    h = x_ref[...] + mixed
    hn = _rms(h, gmlp_ref[...]).astype(BF16)
    acc = jnp.zeros(h.shape, F32)
    for c in range(D_FF // FF_CHUNK):
        ff = slice(c * FF_CHUNK, (c + 1) * FF_CHUNK)
        a = jnp.dot(hn, wup_ref[:, ff], preferred_element_type=F32)
        a = jnp.square(jnp.maximum(a, 0.0))
        acc = acc + jnp.dot(a.astype(BF16), wdown_ref[ff, :], preferred_element_type=F32)
    y_ref[...] = _rms(h + acc, gfin_ref[...])


def _post(x2d, attn2d, ssm2d, w, tm):
    n = x2d.shape[0]
    row = lambda i: (i, 0)
    return pl.pallas_call(
        _post_kernel,
        grid=(n // tm,),
        in_specs=[pl.BlockSpec((tm, D_MODEL), row), pl.BlockSpec((tm, D_ATTN), row),
                  pl.BlockSpec((tm, D_SSM), row),
                  _const_spec((1, D_ATTN)), _const_spec((D_ATTN, D_MODEL)), _const_spec((D_SSM, D_MODEL)),
                  _const_spec((1, D_MODEL)), _const_spec((D_MODEL, D_FF)), _const_spec((D_FF, D_MODEL)),
                  _const_spec((1, D_MODEL))],
        out_specs=pl.BlockSpec((tm, D_MODEL), row),
        out_shape=jax.ShapeDtypeStruct((n, D_MODEL), F32),
        compiler_params=pltpu.CompilerParams(dimension_semantics=("parallel",),
                                             vmem_limit_bytes=VMEM_LIMIT),
        name="post",
    )(x2d, attn2d, ssm2d, w["g_attn_out"], w["w_out_attn"], w["w_out_ssm"], w["g_mlp"], w["w_up"],
      w["w_down"], w["g_final"])


def _rope_tables(pos):
    t = pos.shape[0]
    inv_freq = ROPE_THETA ** (-(jnp.arange(ROPE_HALF, dtype=F32) * 2.0) / QK_ROPE)
    ang = pos.astype(F32)[:, None] * inv_freq[None, :]
    cc = jnp.tile(jnp.cos(ang), (1, 2))
    ss = jnp.tile(jnp.sin(ang), (1, 2))
    pad_hi = jnp.zeros((t, HEAD_PAD - QK_NOPE - QK_ROPE), F32)
    qs = SOFTMAX_SCALE * LOG2_E
    cosq = jnp.concatenate([jnp.full((t, QK_NOPE), qs, F32), cc * qs, pad_hi], axis=1)
    z_half = jnp.zeros((t, ROPE_HALF), F32)
    sin_h = jnp.sin(ang) * qs
    sinq_up = jnp.concatenate([jnp.zeros((t, QK_NOPE), F32), z_half, sin_h, pad_hi], axis=1)
    sinq_dn = jnp.concatenate([jnp.zeros((t, QK_NOPE), F32), -sin_h, z_half, pad_hi], axis=1)
    pad_k = jnp.zeros((t, LANES - QK_ROPE), F32)
    cosk = jnp.concatenate([cc, pad_k], axis=1)
    sink = jnp.concatenate([ss, pad_k], axis=1)
    return cosq, sinq_up, sinq_dn, cosk, sink


def _rot_cols(w_x1, w_x2):
    return -w_x2, w_x1


def _layer_weights(g_mix, w_in, g_q_a, w_q_up, g_kv_a, w_kv_up, a_re, a_im, log_step, b_re, b_im,
                   c_re, c_im, d_skip, w_glu, g_attn_out, g_ssm_out, w_out, g_mlp, w_up, w_down, g_final):
    w = {}
    w["g_mix"] = g_mix[None, :]
    w_cq, w_ckv = w_in[:, :Q_LORA], w_in[:, Q_LORA:Q_LORA + KV_LORA]
    w_kpe = w_in[:, Q_LORA + KV_LORA:Q_LORA + KV_LORA + QK_ROPE]
    w_u = w_in[:, Q_LORA + KV_LORA + QK_ROPE:]
    rot1, rot2 = _rot_cols(w_kpe[:, :ROPE_HALF], w_kpe[:, ROPE_HALF:])
    zk = jnp.zeros((D_MODEL, LANES - QK_ROPE), F32)
    w["w_in_ext"] = jnp.concatenate([w_cq, w_ckv, w_u, w_kpe, zk, rot1, rot2, zk], axis=1).astype(BF16)

    w["g_q_a"] = g_q_a[None, :]
    wq = w_q_up.reshape(Q_LORA, N_HEADS, QK_NOPE + QK_ROPE)
    nope, r1, r2 = wq[:, :, :QK_NOPE], wq[:, :, QK_NOPE:QK_NOPE + ROPE_HALF], wq[:, :, QK_NOPE + ROPE_HALF:]
    zq = jnp.zeros((Q_LORA, N_HEADS, HEAD_PAD - QK_NOPE - QK_ROPE), F32)
    w["w_q_pad"] = jnp.concatenate([nope, r1, r2, zq], axis=2).reshape(Q_LORA, D_HEADS_PAD).astype(BF16)

    w["g_kv_a"] = g_kv_a[None, :]
    wkv = w_kv_up.reshape(KV_LORA, N_HEADS, QK_NOPE + V_DIM)
    zkv = jnp.zeros((KV_LORA, N_HEADS, HEAD_PAD - QK_NOPE), F32)
    wk_pad = jnp.concatenate([wkv[:, :, :QK_NOPE], zkv], axis=2).reshape(KV_LORA, D_HEADS_PAD)
    wv_pad = jnp.concatenate([wkv[:, :, QK_NOPE:], zkv], axis=2).reshape(KV_LORA, D_HEADS_PAD)
    w["w_k_pad"] = wk_pad.astype(BF16)
    w["w_v_t"] = wv_pad.T.astype(BF16)
    wk_t = jnp.transpose(wkv[:, :, :QK_NOPE], (1, 2, 0))
    q_abs = jnp.zeros((N_HEADS, HEAD_PAD, D_KCAT), F32).at[:, :QK_NOPE, :KV_LORA].set(wk_t)
    q_abs = q_abs.at[:, QK_NOPE + jnp.arange(QK_ROPE), KV_LORA + jnp.arange(QK_ROPE)].set(1.0)
    w["w_q_abs"] = q_abs.astype(BF16)
    wv_h = jnp.transpose(wkv[:, :, QK_NOPE:], (1, 0, 2))
    w["w_v_heads"] = jnp.einsum("hce,hg->hcge", wv_h, jnp.eye(N_HEADS, dtype=F32)).reshape(
        N_HEADS, KV_LORA, D_ATTN).astype(BF16)

    lam_re, lam_im, bb_re, bb_im = _ssm_prep(a_re, a_im, log_step, b_re, b_im)
    w["lam_re"] = lam_re.reshape(1, N_STATE)
    w["lam_im"] = lam_im.reshape(1, N_STATE)
    gpc = SSM_CH // SSM_GROUP
    eye = jnp.eye(gpc, dtype=F32)
    by_chunk = lambda m: m.reshape(N_SSM_CHUNKS, gpc, SSM_GROUP, SSM_STATE)
    blk = lambda m: jnp.einsum("cgpn,gh->cgphn", by_chunk(m), eye).reshape(N_SSM_CHUNKS, SSM_CH, SSM_ST)
    w["b_chunks"] = jnp.concatenate([blk(bb_re), blk(bb_im)], axis=2).astype(BF16)
    blk_t = lambda m: jnp.einsum("cgpn,gh->cgnhp", by_chunk(m), eye).reshape(N_SSM_CHUNKS, SSM_ST, SSM_CH)
    w["c_chunks"] = jnp.concatenate([blk_t(c_re), blk_t(-c_im)], axis=1).astype(BF16)
    w["d_skip"] = d_skip[None, :]
    w["w_glu"] = w_glu.astype(BF16)
    w["g_ssm_out"] = g_ssm_out[None, :]

    w["g_attn_out"] = g_attn_out[None, :]
    w["w_out_attn"] = w_out[:D_ATTN].astype(BF16)
    w["w_out_ssm"] = w_out[D_ATTN:].astype(BF16)
    w["g_mlp"] = g_mlp[None, :]
    w["w_up"] = w_up.astype(BF16)
    w["w_down"] = w_down.astype(BF16)
    w["g_final"] = g_final[None, :]
    return w


def _pack_state(h_re, h_im):
    nb = h_re.shape[0]
    return jnp.concatenate([h_re.reshape(nb, N_STATE), h_im.reshape(nb, N_STATE)], axis=1)


def _unpack_state(h):
    nb = h.shape[0]
    return (h[:, :N_STATE].reshape(nb, N_GROUPS, SSM_STATE), h[:, N_STATE:].reshape(nb, N_GROUPS, SSM_STATE))


def _branch(x, pos, past, h0, w, *, proj_tm, post_tm, ssm_tt):
    b, t, _ = x.shape
    n = b * t
    x2d = x.reshape(n, D_MODEL)
    tabs = _rope_tables(pos)
    if t < proj_tm:
        assert proj_tm % t == 0
        tabs = tuple(jnp.tile(a, (proj_tm // t, 1)) for a in tabs)
    else:
        assert t % proj_tm == 0
    if past is None:
        tq = min(ATTN_TQ, t)
        q, lat, kr, u, kpad, vt = _proj(x2d, tabs, w, proj_tm, t, tq)
        attn = _attn_prompt(q, kpad, vt, tq)
    else:
        q, lat, kr, u = _proj(x2d, tabs, w, proj_tm, t, None)
        past_lat, past_kr = past
        plen = past_lat.shape[1]
        assert plen % CHUNK == 0 and t <= CHUNK
        lane_pad = lambda a: jnp.pad(a, ((0, 0), (0, 0), (0, LANES - QK_ROPE))).astype(BF16)
        attn = _attn_sample(q.reshape(b, t, D_HEADS_PAD), past_lat, lane_pad(past_kr),
                            lat.reshape(b, t, KV_LORA), lane_pad(kr.reshape(b, t, QK_ROPE)), w,
                            min(ATTN_TK_SAMPLE, plen))
    y_ssm, h_fin = _ssm(u.reshape(b, t, D_SSM), h0, w, min(ssm_tt, t))
    ssm2d = y_ssm.reshape(n, D_SSM)
    y = _post(x2d, attn.reshape(n, D_ATTN), ssm2d, w, post_tm)
    h_re, h_im = _unpack_state(h_fin)
    return (y.reshape(b, t, D_MODEL), lat.reshape(1, b, t, KV_LORA), kr.reshape(1, b, t, QK_ROPE),
            h_re[None], h_im[None])


def kernel(x_prompt, x_sample, cache_kv_latent, cache_k_rope, state_ssm_re, state_ssm_im, g_mix, w_in, g_q_a,
           w_q_up, g_kv_a, w_kv_up, a_re, a_im, log_step, b_re, b_im, c_re, c_im, d_skip, w_glu, g_attn_out,
           g_ssm_out, w_out, g_mlp, w_up, w_down, g_final):
    assert g_mix.shape[0] == 1, "single-layer trunk"
    w = _layer_weights(g_mix[0], w_in[0], g_q_a[0], w_q_up[0], g_kv_a[0], w_kv_up[0], a_re[0], a_im[0],
                       log_step[0], b_re[0], b_im[0], c_re[0], c_im[0], d_skip[0], w_glu[0], g_attn_out[0],
                       g_ssm_out[0], w_out[0], g_mlp[0], w_up[0], w_down[0], g_final)
    tiles = dict(proj_tm=PROJ_TM, post_tm=POST_TM, ssm_tt=SSM_TT)

    bp, tp, _ = x_prompt.shape
    pos_p = jnp.arange(tp, dtype=jnp.int32)
    h0p = jnp.zeros((bp, 2 * N_STATE), F32)
    y_p, lat_p, kr_p, hr_p, hi_p = _branch(x_prompt, pos_p, None, h0p, w, **tiles)

    bs, ts, _ = x_sample.shape
    plen = cache_kv_latent.shape[2]
    pos_s = plen + jnp.arange(ts, dtype=jnp.int32)
    h0s = _pack_state(state_ssm_re[0], state_ssm_im[0])
    y_s, lat_s, kr_s, hr_s, hi_s = _branch(x_sample, pos_s, (cache_kv_latent[0], cache_k_rope[0]), h0s, w, **tiles)
    return (y_p, y_s, lat_p, kr_p, hr_p, hi_p, lat_s, kr_s, hr_s, hi_s)
```

```python
import functools
import math

import jax
import jax.numpy as jnp
from jax import lax
from jax.experimental import pallas as pl
from jax.experimental.pallas import tpu as pltpu

F32 = jnp.float32
BF16 = jnp.bfloat16

D_MODEL = 1024
N_HEADS = 8
QK_NOPE = 64
QK_ROPE = 32
ROPE_HALF = QK_ROPE // 2
V_DIM = 64
KV_LORA = 256
Q_LORA = 768
D_ATTN = N_HEADS * V_DIM
D_SSM = 512
SSM_GROUP = 16
N_GROUPS = D_SSM // SSM_GROUP
SSM_STATE = 64
N_STATE = N_GROUPS * SSM_STATE
D_FF = 4 * D_MODEL
CHUNK = 64
ROPE_THETA = 10000.0
SOFTMAX_SCALE = (QK_NOPE + QK_ROPE) ** -0.5
LOG2_E = math.log2(math.e)
EPS = 1e-6
NEG_INF = -1e30

LANES = 128
SUBLANES = 8
HEAD_PAD = LANES
D_HEADS_PAD = N_HEADS * HEAD_PAD
COL_CQ = 0
COL_CKV = Q_LORA
COL_U = Q_LORA + KV_LORA
COL_KPE = COL_U + D_SSM
COL_KPE_ROT = COL_KPE + LANES
D_IN_EXT = COL_KPE_ROT + LANES

VMEM_LIMIT = 56 * 1024 * 1024

PROJ_TM = 512
ATTN_TQ = 512
ATTN_TK_SAMPLE = 512
ATTN_SUB = 2
SSM_TT = 64
POST_TM = 512
FF_CHUNK = 1024


def _const_spec(shape):
    nd = len(shape)
    return pl.BlockSpec(shape, lambda *_: (0,) * nd, pipeline_mode=pl.Buffered(1))


def _rms(x, g):
    return x * lax.rsqrt(jnp.mean(x * x, axis=-1, keepdims=True) + EPS) * g


def _prep_kernel(are_ref, aim_ref, ls_ref, bre_ref, bim_ref, lre_ref, lim_ref, bbre_ref, bbim_ref):
    dt = jnp.exp(ls_ref[...])
    lr, li = are_ref[...], aim_ref[...]
    mag = jnp.exp(lr * dt)
    lb_re, lb_im = mag * jnp.cos(li * dt), mag * jnp.sin(li * dt)
    nr, ni = lb_re - 1.0, lb_im
    den = lr * lr + li * li
    coef_re = (nr * lr + ni * li) / den
    coef_im = (ni * lr - nr * li) / den
    lre_ref[...] = lb_re
    lim_ref[...] = lb_im
    bre, bim = bre_ref[...], bim_ref[...]
    cr, ci = coef_re[:, None, :], coef_im[:, None, :]
    bbre_ref[...] = cr * bre - ci * bim
    bbim_ref[...] = cr * bim + ci * bre


def _ssm_prep(a_re, a_im, log_step, b_re, b_im):
    g, n = a_re.shape
    p = b_re.shape[-1]
    bre_t = jnp.swapaxes(b_re, 1, 2)
    bim_t = jnp.swapaxes(b_im, 1, 2)
    return pl.pallas_call(
        _prep_kernel,
        out_shape=(jax.ShapeDtypeStruct((g, n), F32), jax.ShapeDtypeStruct((g, n), F32),
                   jax.ShapeDtypeStruct((g, p, n), F32), jax.ShapeDtypeStruct((g, p, n), F32)),
        name="ssm_prep",
    )(a_re, a_im, log_step.reshape(g, 1), bre_t, bim_t)


def _proj_kernel(x_ref, cq_ref, sqa_ref, sqb_ref, ck_ref, sk_ref, gmix_ref, win_ref, gq_ref, wq_ref, gkv_ref,
                 wk_ref, wvt_ref,
                 q_out, lat_out, kr_out, u_out, *kv_out, v_key_tile):
    xn = _rms(x_ref[...], gmix_ref[...]).astype(BF16)
    proj = jnp.dot(xn, win_ref[...], preferred_element_type=F32)
    u_out[...] = proj[:, COL_U:COL_U + D_SSM]

    cqn = _rms(proj[:, COL_CQ:COL_CQ + Q_LORA], gq_ref[...]).astype(BF16)
    q = jnp.dot(cqn, wq_ref[...], preferred_element_type=F32)
    q_up = pltpu.roll(q, ROPE_HALF, 1)
    q_dn = pltpu.roll(q, D_HEADS_PAD - ROPE_HALF, 1)
    cq_t, sqa_t, sqb_t = cq_ref[...], sqa_ref[...], sqb_ref[...]
    pair_major = v_key_tile is not None

    def put_head(ref, h, tile):
        if pair_major:
            ref[h // 2, :, (h % 2) * HEAD_PAD:(h % 2 + 1) * HEAD_PAD] = tile
        else:
            ref[:, h * HEAD_PAD:(h + 1) * HEAD_PAD] = tile

    for h in range(N_HEADS):
        sl = slice(h * HEAD_PAD, (h + 1) * HEAD_PAD)
        put_head(q_out, h, (q[:, sl] * cq_t + q_up[:, sl] * sqa_t + q_dn[:, sl] * sqb_t).astype(BF16))

    lat = _rms(proj[:, COL_CKV:COL_CKV + KV_LORA], gkv_ref[...])
    lat_out[...] = lat
    kr = (proj[:, COL_KPE:COL_KPE + LANES] * ck_ref[...]
          + proj[:, COL_KPE_ROT:COL_KPE_ROT + LANES] * sk_ref[...])
    kr_out[...] = kr[:, :QK_ROPE]
    if v_key_tile is None:
        return
    kpad_out, vt_out = kv_out
    latb = lat.astype(BF16)
    k_nope = jnp.dot(latb, wk_ref[...], preferred_element_type=F32)
    kr_placed = pltpu.roll(kr, QK_NOPE, 1)
    for h in range(N_HEADS):
        put_head(kpad_out, h, (k_nope[:, h * HEAD_PAD:(h + 1) * HEAD_PAD] + kr_placed).astype(BF16))
    vt = lax.dot_general(wvt_ref[...], latb, (((1,), (1,)), ((), ())),
                         preferred_element_type=F32)
    ones = lax.broadcasted_iota(jnp.int32, vt.shape, 0) % HEAD_PAD >= V_DIM
    vt = jnp.where(ones, 1.0, vt).astype(BF16)
    for s in range(vt.shape[1] // v_key_tile):
        vt_out[s] = vt[:, s * v_key_tile:(s + 1) * v_key_tile]


def _proj(x2d, tabs, w, tm, stream_len, v_key_tile):
    n = x2d.shape[0]
    cosq, sinq_up, sinq_dn, cosk, sink = tabs
    n_tab = cosq.shape[0] // tm
    row = lambda i: (i, 0)
    tab = lambda i: (i % n_tab, 0)
    rest_shape = [
        jax.ShapeDtypeStruct((n, KV_LORA), F32),
        jax.ShapeDtypeStruct((n, QK_ROPE), F32),
        jax.ShapeDtypeStruct((n, D_SSM), F32),
    ]
    rest_specs = [pl.BlockSpec((tm, KV_LORA), row), pl.BlockSpec((tm, QK_ROPE), row), pl.BlockSpec((tm, D_SSM), row)]
    if v_key_tile is None:
        out_shape = [jax.ShapeDtypeStruct((n, D_HEADS_PAD), BF16)] + rest_shape
        out_specs = [pl.BlockSpec((tm, D_HEADS_PAD), row)] + rest_specs
    else:
        assert stream_len % tm == 0 and tm % v_key_tile == 0
        tiles_per_stream = stream_len // tm
        n_streams = n // stream_len
        pair_shape = jax.ShapeDtypeStruct((n_streams, N_HEADS // 2, stream_len, 2 * HEAD_PAD), BF16)
        pair_spec = pl.BlockSpec((None, N_HEADS // 2, tm, 2 * HEAD_PAD),
                                 lambda i: (i // tiles_per_stream, 0, i % tiles_per_stream, 0))
        out_shape = [pair_shape] + rest_shape + [
            pair_shape,
            jax.ShapeDtypeStruct((n_streams, stream_len // v_key_tile, D_HEADS_PAD, v_key_tile), BF16)]
        out_specs = [pair_spec] + rest_specs + [
            pair_spec,
            pl.BlockSpec((None, tm // v_key_tile, D_HEADS_PAD, v_key_tile),
                         lambda i: (i // tiles_per_stream, i % tiles_per_stream, 0, 0))]
    return pl.pallas_call(
        functools.partial(_proj_kernel, v_key_tile=v_key_tile),
        grid=(n // tm,),
        in_specs=[
            pl.BlockSpec((tm, D_MODEL), row),
            pl.BlockSpec((tm, LANES), tab), pl.BlockSpec((tm, LANES), tab), pl.BlockSpec((tm, LANES), tab),
            pl.BlockSpec((tm, LANES), tab), pl.BlockSpec((tm, LANES), tab),
            _const_spec((1, D_MODEL)), _const_spec((D_MODEL, D_IN_EXT)),
            _const_spec((1, Q_LORA)), _const_spec((Q_LORA, D_HEADS_PAD)),
            _const_spec((1, KV_LORA)), _const_spec((KV_LORA, D_HEADS_PAD)),
            _const_spec((D_HEADS_PAD, KV_LORA)),
        ],
        out_specs=tuple(out_specs),
        out_shape=tuple(out_shape),
        compiler_params=pltpu.CompilerParams(dimension_semantics=("parallel",),
                                             vmem_limit_bytes=VMEM_LIMIT),
        name="proj",
    )(x2d, cosq, sinq_up, sinq_dn, cosk, sink, w["g_mix"], w["w_in_ext"], w["g_q_a"], w["w_q_pad"], w["g_kv_a"],
      w["w_k_pad"], w["w_v_t"])


def _attn_prompt_kernel(q_ref, k_ref, vt_ref, o_ref, s00, s01, s10, s11, mx_s, *, tq):
    i = pl.program_id(2)
    key_chunk = lax.broadcasted_iota(jnp.int32, (tq, tq), 0) // CHUNK
    query_chunk = lax.broadcasted_iota(jnp.int32, (tq, tq), 1) // CHUNK
    diag_mask = key_chunk <= query_chunk
    heads = [slice(h * HEAD_PAD, (h + 1) * HEAD_PAD) for h in range(2)]
    slots = ((s00, s01), (s10, s11))
    n_sub = ATTN_SUB
    ts = tq // n_sub

    def scores(j, slot):
        off = pl.multiple_of(j * tq, tq)
        for h, (sl, buf) in enumerate(zip(heads, slots[slot])):
            st = lax.dot_general(k_ref[pl.ds(off, tq), sl], q_ref[:, sl], (((1,), (1,)), ((), ())),
                                 preferred_element_type=F32)
            buf[...] = st
            for s in range(n_sub):
                mx_s[slot, h, s] = jnp.max(st[s * ts:(s + 1) * ts], axis=0, keepdims=True)

    def consume(j, slot, state, mask):
        for s in range(n_sub):
            rows = slice(s * ts, (s + 1) * ts)
            new_state = []
            for h, (sl, buf, (m, acc)) in enumerate(zip(heads, slots[slot], state)):
                st = buf[rows, :]
                if mask is not None:
                    st = jnp.where(mask[rows], st, NEG_INF)
                    m_new = jnp.maximum(m, jnp.max(st, axis=0, keepdims=True))
                else:
                    m_new = jnp.maximum(m, mx_s[slot, h, s])
                alpha = jnp.exp2(m - m_new)
                p = jnp.exp2(st - m_new).astype(BF16)
                acc = acc * alpha + jnp.dot(vt_ref[j, sl, rows], p, preferred_element_type=F32)
                new_state.append((m_new, acc))
            state = tuple(new_state)
        return state

    def pair(p, state):
        scores(2 * p + 1, 1)
        state = consume(2 * p, 0, state, None)
        scores(2 * p + 2, 0)
        return consume(2 * p + 1, 1, state, None)

    init = tuple((jnp.full((1, tq), NEG_INF, F32), jnp.zeros((HEAD_PAD, tq), F32)) for _ in heads)
    scores(0, 0)
    state = lax.fori_loop(0, i // 2, pair, init)

    def even_tail(state):
        return consume(i, 0, state, diag_mask)

    def odd_tail(state):
        scores(i, 1)
        state = consume(i - 1, 0, state, None)
        return consume(i, 1, state, diag_mask)

    state = lax.cond(i % 2 == 0, even_tail, odd_tail, state)
    o_t = jnp.concatenate([acc[:V_DIM] / acc[V_DIM:V_DIM + 1] for _, acc in state], axis=0)
    o_ref[...] = o_t.T


def _attn_prompt(q, kpad, vt, tq):
    b, _, t, _ = q.shape
    assert tq % CHUNK == 0 and t % tq == 0 and vt.shape == (b, t // tq, D_HEADS_PAD, tq)
    return pl.pallas_call(
        functools.partial(_attn_prompt_kernel, tq=tq),
        grid=(b, N_HEADS // 2, t // tq),
        in_specs=[pl.BlockSpec((None, None, tq, 2 * HEAD_PAD), lambda bi, hp, i: (bi, hp, i, 0)),
                  pl.BlockSpec((None, None, t, 2 * HEAD_PAD), lambda bi, hp, i: (bi, hp, 0, 0)),
                  pl.BlockSpec((None, t // tq, 2 * HEAD_PAD, tq), lambda bi, hp, i: (bi, 0, hp, 0))],
        out_specs=pl.BlockSpec((None, tq, 2 * V_DIM), lambda bi, hp, i: (bi, i, hp)),
        out_shape=jax.ShapeDtypeStruct((b, t, D_ATTN), F32),
        scratch_shapes=[pltpu.VMEM((tq, tq), F32)] * 4 + [pltpu.VMEM((2, 2, ATTN_SUB, 1, tq), F32)],
        compiler_params=pltpu.CompilerParams(
            dimension_semantics=("parallel", "parallel", "arbitrary"), vmem_limit_bytes=VMEM_LIMIT),
        name="attn_prompt",
    )(q, kpad, vt)


D_KCAT = KV_LORA + LANES


def _attn_sample_kernel(q_ref, plat_ref, pkr_ref, nlat_ref, nkr_ref, wabs_ref, wv_ref, o_ref, s0, s1, sn, *, tk):
    tq = q_ref.shape[0]
    n_past = plat_ref.shape[0] // tk
    qcat = jnp.concatenate(
        [jnp.dot(q_ref[:, h * HEAD_PAD:(h + 1) * HEAD_PAD], wabs_ref[h], preferred_element_type=F32)
         for h in range(N_HEADS)], axis=0).astype(BF16)
    tiles = [(plat_ref.at[pl.ds(j * tk, tk)], pkr_ref.at[pl.ds(j * tk, tk)], (s0, s1)[j % 2])
             for j in range(n_past)] + [(nlat_ref, nkr_ref, sn)]

    def scores(lat_ref, kr_ref, buf):
        kcat = jnp.concatenate([lat_ref[...].astype(BF16), kr_ref[...]], axis=1)
        buf[...] = lax.dot_general(qcat, kcat, (((1,), (1,)), ((), ())), preferred_element_type=F32)

    def consume(lat_ref, buf, carry):
        keys = buf.shape[1]
        ts = min(keys, tk // ATTN_SUB)
        for c in range(keys // ts):
            m, l, acc = carry
            s = buf[:, c * ts:(c + 1) * ts]
            m_new = jnp.maximum(m, jnp.max(s, axis=-1, keepdims=True))
            alpha = jnp.exp2(m - m_new)
            p = jnp.exp2(s - m_new)
            l = l * alpha + jnp.sum(p, axis=-1, keepdims=True)
            acc = acc * alpha + jnp.dot(p.astype(BF16), lat_ref[c * ts:(c + 1) * ts, :].astype(BF16),
                                        preferred_element_type=F32)
            carry = (m_new, l, acc)
        return carry

    rows = N_HEADS * tq
    carry = (jnp.full((rows, 1), NEG_INF, F32), jnp.zeros((rows, 1), F32), jnp.zeros((rows, KV_LORA), F32))
    scores(*tiles[0])
    for idx, (lat_ref, _, buf) in enumerate(tiles):
        if idx + 1 < len(tiles):
            scores(*tiles[idx + 1])
        carry = consume(lat_ref, buf, carry)
    _, l, acc = carry
    o_lat = (acc / l).astype(BF16)
    out = jnp.zeros((tq, D_ATTN), F32)
    for h in range(N_HEADS):
        out = out + jnp.dot(o_lat[h * tq:(h + 1) * tq], wv_ref[h], preferred_element_type=F32)
    o_ref[...] = out


def _attn_sample(q, past_lat, past_kr, new_lat, new_kr, w, tk):
    b, tq, _ = q.shape
    past = past_lat.shape[1]
    assert past % tk == 0
    blk = lambda bi: (bi, 0, 0)
    return pl.pallas_call(
        functools.partial(_attn_sample_kernel, tk=tk),
        grid=(b,),
        in_specs=[pl.BlockSpec((None, tq, D_HEADS_PAD), blk),
                  pl.BlockSpec((None, past, KV_LORA), blk), pl.BlockSpec((None, past, LANES), blk),
                  pl.BlockSpec((None, tq, KV_LORA), blk), pl.BlockSpec((None, tq, LANES), blk),
                  _const_spec((N_HEADS, HEAD_PAD, D_KCAT)), _const_spec((N_HEADS, KV_LORA, D_ATTN))],
        out_specs=pl.BlockSpec((None, tq, D_ATTN), blk),
        out_shape=jax.ShapeDtypeStruct((b, tq, D_ATTN), F32),
        scratch_shapes=[pltpu.VMEM((N_HEADS * tq, tk), F32), pltpu.VMEM((N_HEADS * tq, tk), F32),
                        pltpu.VMEM((N_HEADS * tq, tq), F32)],
        compiler_params=pltpu.CompilerParams(dimension_semantics=("parallel",),
                                             vmem_limit_bytes=VMEM_LIMIT),
        name="attn_sample",
    )(q, past_lat, past_kr, new_lat, new_kr, w["w_q_abs"], w["w_v_heads"])


SSM_CH = LANES
SSM_ST = SSM_CH // SSM_GROUP * SSM_STATE
N_SSM_CHUNKS = D_SSM // SSM_CH


def _ssm_kernel(u_ref, h0_ref, lre_ref, lim_ref, bc_ref, cc_ref, d_ref, wglu_ref, gssm_ref,
                y_ref, hout_ref, x_s, h_s, hc_s, *, tt):
    ti = pl.program_id(1)

    @pl.when(ti == 0)
    def _():
        hc_s[...] = h0_ref[...]

    rows = tt * SUBLANES
    u = jnp.swapaxes(u_ref[...], 0, 1).reshape(rows, D_SSM)
    ub = u.astype(BF16)
    chunks = []
    for c in range(N_SSM_CHUNKS):
        re = slice(c * SSM_ST, (c + 1) * SSM_ST)
        im = slice(N_STATE + c * SSM_ST, N_STATE + (c + 1) * SSM_ST)
        chunks.append((re, im))
        xc = jnp.dot(ub[:, c * SSM_CH:(c + 1) * SSM_CH], bc_ref[c], preferred_element_type=F32)
        x_s[:, :, re] = xc[:, :SSM_ST].reshape(tt, SUBLANES, SSM_ST)
        x_s[:, :, im] = xc[:, SSM_ST:].reshape(tt, SUBLANES, SSM_ST)

    for re, im in chunks:
        lr = jnp.broadcast_to(lre_ref[:, re], (SUBLANES, SSM_ST))
        li = jnp.broadcast_to(lim_ref[:, re], (SUBLANES, SSM_ST))

        def step(t, carry, re=re, im=im, lr=lr, li=li):
            hr, hi = carry
            nhr = lr * hr - li * hi + x_s[t, :, re]
            nhi = lr * hi + li * hr + x_s[t, :, im]
            h_s[t, :, re] = nhr
            h_s[t, :, im] = nhi
            return nhr, nhi

        hr, hi = lax.fori_loop(0, tt, step, (hc_s[:, re], hc_s[:, im]), unroll=True)
        hc_s[:, re] = hr
        hc_s[:, im] = hi

    ys = []
    for c, (re, im) in enumerate(chunks):
        hcat = jnp.concatenate([h_s[:, :, re], h_s[:, :, im]], axis=-1).reshape(rows, 2 * SSM_ST)
        ys.append(jnp.dot(hcat.astype(BF16), cc_ref[c], preferred_element_type=F32))
    y = jnp.concatenate(ys, axis=1) + d_ref[...] * u
    y = jax.nn.gelu(y, approximate=True)
    z = jnp.dot(y.astype(BF16), wglu_ref[...], preferred_element_type=F32)
    y = y * (1.0 / (1.0 + jnp.exp(-z)))
    y_ref[...] = jnp.swapaxes(_rms(y, gssm_ref[...]).reshape(tt, SUBLANES, D_SSM), 0, 1)

    @pl.when(ti == pl.num_programs(1) - 1)
    def _():
        hout_ref[...] = hc_s[...]


def _ssm(u, h0, w, tt):
    nb, t, _ = u.shape
    assert nb % SUBLANES == 0 and t % tt == 0
    return pl.pallas_call(
        functools.partial(_ssm_kernel, tt=tt),
        grid=(nb // SUBLANES, t // tt),
        in_specs=[pl.BlockSpec((SUBLANES, tt, D_SSM), lambda bg, ti: (bg, ti, 0)),
                  pl.BlockSpec((SUBLANES, 2 * N_STATE), lambda bg, ti: (bg, 0)),
                  _const_spec((1, N_STATE)), _const_spec((1, N_STATE)),
                  _const_spec((N_SSM_CHUNKS, SSM_CH, 2 * SSM_ST)), _const_spec((N_SSM_CHUNKS, 2 * SSM_ST, SSM_CH)),
                  _const_spec((1, D_SSM)), _const_spec((D_SSM, D_SSM)), _const_spec((1, D_SSM))],
        out_specs=(pl.BlockSpec((SUBLANES, tt, D_SSM), lambda bg, ti: (bg, ti, 0)),
                   pl.BlockSpec((SUBLANES, 2 * N_STATE), lambda bg, ti: (bg, 0))),
        out_shape=(jax.ShapeDtypeStruct((nb, t, D_SSM), F32),
                   jax.ShapeDtypeStruct((nb, 2 * N_STATE), F32)),
        scratch_shapes=[pltpu.VMEM((tt, SUBLANES, 2 * N_STATE), F32),
                        pltpu.VMEM((tt, SUBLANES, 2 * N_STATE), F32),
                        pltpu.VMEM((SUBLANES, 2 * N_STATE), F32)],
        compiler_params=pltpu.CompilerParams(dimension_semantics=("parallel", "arbitrary"),
                                             vmem_limit_bytes=VMEM_LIMIT),
        name="ssm",
    )(u, h0, w["lam_re"], w["lam_im"], w["b_chunks"], w["c_chunks"], w["d_skip"], w["w_glu"],
      w["g_ssm_out"])


def _post_kernel(x_ref, attn_ref, ssm_ref, gattn_ref, wouta_ref, wouts_ref, gmlp_ref, wup_ref, wdown_ref,
                 gfin_ref, y_ref):
    an = _rms(attn_ref[...], gattn_ref[...]).astype(BF16)
    mixed = (jnp.dot(an, wouta_ref[...], preferred_element_type=F32)
             + jnp.dot(ssm_ref[...].astype(BF16), wouts_ref[...], preferred_element_type=F32))
    h = x_ref[...] + mixed
    hn = _rms(h, gmlp_ref[...]).astype(BF16)
    acc = jnp.zeros(h.shape, F32)
    for c in range(D_FF // FF_CHUNK):
        ff = slice(c * FF_CHUNK, (c + 1) * FF_CHUNK)
        a = jnp.dot(hn, wup_ref[:, ff], preferred_element_type=F32)
        a = jnp.square(jnp.maximum(a, 0.0))
        acc = acc + jnp.dot(a.astype(BF16), wdown_ref[ff, :], preferred_element_type=F32)
    y_ref[...] = _rms(h + acc, gfin_ref[...])


def _post(x2d, attn2d, ssm2d, w, tm):
    n = x2d.shape[0]
    row = lambda i: (i, 0)
    return pl.pallas_call(
        _post_kernel,
        grid=(n // tm,),
        in_specs=[pl.BlockSpec((tm, D_MODEL), row), pl.BlockSpec((tm, D_ATTN), row),
                  pl.BlockSpec((tm, D_SSM), row),
                  _const_spec((1, D_ATTN)), _const_spec((D_ATTN, D_MODEL)), _const_spec((D_SSM, D_MODEL)),
                  _const_spec((1, D_MODEL)), _const_spec((D_MODEL, D_FF)), _const_spec((D_FF, D_MODEL)),
                  _const_spec((1, D_MODEL))],
        out_specs=pl.BlockSpec((tm, D_MODEL), row),
        out_shape=jax.ShapeDtypeStruct((n, D_MODEL), F32),
        compiler_params=pltpu.CompilerParams(dimension_semantics=("parallel",),
                                             vmem_limit_bytes=VMEM_LIMIT),
        name="post",
    )(x2d, attn2d, ssm2d, w["g_attn_out"], w["w_out_attn"], w["w_out_ssm"], w["g_mlp"], w["w_up"],
      w["w_down"], w["g_final"])


def _rope_tables(pos):
    t = pos.shape[0]
    inv_freq = ROPE_THETA ** (-(jnp.arange(ROPE_HALF, dtype=F32) * 2.0) / QK_ROPE)
    ang = pos.astype(F32)[:, None] * inv_freq[None, :]
    cc = jnp.tile(jnp.cos(ang), (1, 2))
    ss = jnp.tile(jnp.sin(ang), (1, 2))
    pad_hi = jnp.zeros((t, HEAD_PAD - QK_NOPE - QK_ROPE), F32)
    qs = SOFTMAX_SCALE * LOG2_E
    cosq = jnp.concatenate([jnp.full((t, QK_NOPE), qs, F32), cc * qs, pad_hi], axis=1)
    z_half = jnp.zeros((t, ROPE_HALF), F32)
    sin_h = jnp.sin(ang) * qs
    sinq_up = jnp.concatenate([jnp.zeros((t, QK_NOPE), F32), z_half, sin_h, pad_hi], axis=1)
    sinq_dn = jnp.concatenate([jnp.zeros((t, QK_NOPE), F32), -sin_h, z_half, pad_hi], axis=1)
    pad_k = jnp.zeros((t, LANES - QK_ROPE), F32)
    cosk = jnp.concatenate([cc, pad_k], axis=1)
    sink = jnp.concatenate([ss, pad_k], axis=1)
    return cosq, sinq_up, sinq_dn, cosk, sink


def _rot_cols(w_x1, w_x2):
    return -w_x2, w_x1


def _layer_weights(g_mix, w_in, g_q_a, w_q_up, g_kv_a, w_kv_up, a_re, a_im, log_step, b_re, b_im,
                   c_re, c_im, d_skip, w_glu, g_attn_out, g_ssm_out, w_out, g_mlp, w_up, w_down, g_final):
    w = {}
    w["g_mix"] = g_mix[None, :]
    w_cq, w_ckv = w_in[:, :Q_LORA], w_in[:, Q_LORA:Q_LORA + KV_LORA]
    w_kpe = w_in[:, Q_LORA + KV_LORA:Q_LORA + KV_LORA + QK_ROPE]
    w_u = w_in[:, Q_LORA + KV_LORA + QK_ROPE:]
    rot1, rot2 = _rot_cols(w_kpe[:, :ROPE_HALF], w_kpe[:, ROPE_HALF:])
    zk = jnp.zeros((D_MODEL, LANES - QK_ROPE), F32)
    w["w_in_ext"] = jnp.concatenate([w_cq, w_ckv, w_u, w_kpe, zk, rot1, rot2, zk], axis=1).astype(BF16)

    w["g_q_a"] = g_q_a[None, :]
    wq = w_q_up.reshape(Q_LORA, N_HEADS, QK_NOPE + QK_ROPE)
    nope, r1, r2 = wq[:, :, :QK_NOPE], wq[:, :, QK_NOPE:QK_NOPE + ROPE_HALF], wq[:, :, QK_NOPE + ROPE_HALF:]
    zq = jnp.zeros((Q_LORA, N_HEADS, HEAD_PAD - QK_NOPE - QK_ROPE), F32)
    w["w_q_pad"] = jnp.concatenate([nope, r1, r2, zq], axis=2).reshape(Q_LORA, D_HEADS_PAD).astype(BF16)

    w["g_kv_a"] = g_kv_a[None, :]
    wkv = w_kv_up.reshape(KV_LORA, N_HEADS, QK_NOPE + V_DIM)
    zkv = jnp.zeros((KV_LORA, N_HEADS, HEAD_PAD - QK_NOPE), F32)
    wk_pad = jnp.concatenate([wkv[:, :, :QK_NOPE], zkv], axis=2).reshape(KV_LORA, D_HEADS_PAD)
    wv_pad = jnp.concatenate([wkv[:, :, QK_NOPE:], zkv], axis=2).reshape(KV_LORA, D_HEADS_PAD)
    w["w_k_pad"] = wk_pad.astype(BF16)
    w["w_v_t"] = wv_pad.T.astype(BF16)
    wk_t = jnp.transpose(wkv[:, :, :QK_NOPE], (1, 2, 0))
    q_abs = jnp.zeros((N_HEADS, HEAD_PAD, D_KCAT), F32).at[:, :QK_NOPE, :KV_LORA].set(wk_t)
    q_abs = q_abs.at[:, QK_NOPE + jnp.arange(QK_ROPE), KV_LORA + jnp.arange(QK_ROPE)].set(1.0)
    w["w_q_abs"] = q_abs.astype(BF16)
    wv_h = jnp.transpose(wkv[:, :, QK_NOPE:], (1, 0, 2))
    w["w_v_heads"] = jnp.einsum("hce,hg->hcge", wv_h, jnp.eye(N_HEADS, dtype=F32)).reshape(
        N_HEADS, KV_LORA, D_ATTN).astype(BF16)

    lam_re, lam_im, bb_re, bb_im = _ssm_prep(a_re, a_im, log_step, b_re, b_im)
    w["lam_re"] = lam_re.reshape(1, N_STATE)
    w["lam_im"] = lam_im.reshape(1, N_STATE)
    gpc = SSM_CH // SSM_GROUP
    eye = jnp.eye(gpc, dtype=F32)
    by_chunk = lambda m: m.reshape(N_SSM_CHUNKS, gpc, SSM_GROUP, SSM_STATE)
    blk = lambda m: jnp.einsum("cgpn,gh->cgphn", by_chunk(m), eye).reshape(N_SSM_CHUNKS, SSM_CH, SSM_ST)
    w["b_chunks"] = jnp.concatenate([blk(bb_re), blk(bb_im)], axis=2).astype(BF16)
    blk_t = lambda m: jnp.einsum("cgpn,gh->cgnhp", by_chunk(m), eye).reshape(N_SSM_CHUNKS, SSM_ST, SSM_CH)
    w["c_chunks"] = jnp.concatenate([blk_t(c_re), blk_t(-c_im)], axis=1).astype(BF16)
    w["d_skip"] = d_skip[None, :]
    w["w_glu"] = w_glu.astype(BF16)
    w["g_ssm_out"] = g_ssm_out[None, :]

    w["g_attn_out"] = g_attn_out[None, :]
    w["w_out_attn"] = w_out[:D_ATTN].astype(BF16)
    w["w_out_ssm"] = w_out[D_ATTN:].astype(BF16)
    w["g_mlp"] = g_mlp[None, :]
    w["w_up"] = w_up.astype(BF16)
    w["w_down"] = w_down.astype(BF16)
    w["g_final"] = g_final[None, :]
    return w


def _pack_state(h_re, h_im):
    nb = h_re.shape[0]
    return jnp.concatenate([h_re.reshape(nb, N_STATE), h_im.reshape(nb, N_STATE)], axis=1)


def _unpack_state(h):
    nb = h.shape[0]
    return (h[:, :N_STATE].reshape(nb, N_GROUPS, SSM_STATE), h[:, N_STATE:].reshape(nb, N_GROUPS, SSM_STATE))


def _branch(x, pos, past, h0, w, *, proj_tm, post_tm, ssm_tt):
    b, t, _ = x.shape
    n = b * t
    x2d = x.reshape(n, D_MODEL)
    tabs = _rope_tables(pos)
    if t < proj_tm:
        assert proj_tm % t == 0
        tabs = tuple(jnp.tile(a, (proj_tm // t, 1)) for a in tabs)
    else:
        assert t % proj_tm == 0
    if past is None:
        tq = min(ATTN_TQ, t)
        q, lat, kr, u, kpad, vt = _proj(x2d, tabs, w, proj_tm, t, tq)
        attn = _attn_prompt(q, kpad, vt, tq)
    else:
        q, lat, kr, u = _proj(x2d, tabs, w, proj_tm, t, None)
        past_lat, past_kr = past
        plen = past_lat.shape[1]
        assert plen % CHUNK == 0 and t <= CHUNK
        lane_pad = lambda a: jnp.pad(a, ((0, 0), (0, 0), (0, LANES - QK_ROPE))).astype(BF16)
        attn = _attn_sample(q.reshape(b, t, D_HEADS_PAD), past_lat, lane_pad(past_kr),
                            lat.reshape(b, t, KV_LORA), lane_pad(kr.reshape(b, t, QK_ROPE)), w,
                            min(ATTN_TK_SAMPLE, plen))
    y_ssm, h_fin = _ssm(u.reshape(b, t, D_SSM), h0, w, min(ssm_tt, t))
    ssm2d = y_ssm.reshape(n, D_SSM)
    y = _post(x2d, attn.reshape(n, D_ATTN), ssm2d, w, post_tm)
    h_re, h_im = _unpack_state(h_fin)
    return (y.reshape(b, t, D_MODEL), lat.reshape(1, b, t, KV_LORA), kr.reshape(1, b, t, QK_ROPE),
            h_re[None], h_im[None])


def kernel(x_prompt, x_sample, cache_kv_latent, cache_k_rope, state_ssm_re, state_ssm_im, g_mix, w_in, g_q_a,
           w_q_up, g_kv_a, w_kv_up, a_re, a_im, log_step, b_re, b_im, c_re, c_im, d_skip, w_glu, g_attn_out,
           g_ssm_out, w_out, g_mlp, w_up, w_down, g_final):
    assert g_mix.shape[0] == 1, "single-layer trunk"
    w = _layer_weights(g_mix[0], w_in[0], g_q_a[0], w_q_up[0], g_kv_a[0], w_kv_up[0], a_re[0], a_im[0],
                       log_step[0], b_re[0], b_im[0], c_re[0], c_im[0], d_skip[0], w_glu[0], g_attn_out[0],
                       g_ssm_out[0], w_out[0], g_mlp[0], w_up[0], w_down[0], g_final)
    tiles = dict(proj_tm=PROJ_TM, post_tm=POST_TM, ssm_tt=SSM_TT)

    bp, tp, _ = x_prompt.shape
    pos_p = jnp.arange(tp, dtype=jnp.int32)
    h0p = jnp.zeros((bp, 2 * N_STATE), F32)
    y_p, lat_p, kr_p, hr_p, hi_p = _branch(x_prompt, pos_p, None, h0p, w, **tiles)

    bs, ts, _ = x_sample.shape
    plen = cache_kv_latent.shape[2]
    pos_s = plen + jnp.arange(ts, dtype=jnp.int32)
    h0s = _pack_state(state_ssm_re[0], state_ssm_im[0])
    y_s, lat_s, kr_s, hr_s, hi_s = _branch(x_sample, pos_s, (cache_kv_latent[0], cache_k_rope[0]), h0s, w, **tiles)
    return (y_p, y_s, lat_p, kr_p, hr_p, hi_p, lat_s, kr_s, hr_s, hi_s)
```

```python
import functools
import math

import jax
import jax.numpy as jnp
from jax import lax
from jax.experimental import pallas as pl
from jax.experimental.pallas import tpu as pltpu

F32 = jnp.float32
BF16 = jnp.bfloat16

D_MODEL = 1024
N_HEADS = 8
QK_NOPE = 64
QK_ROPE = 32
ROPE_HALF = QK_ROPE // 2
V_DIM = 64
KV_LORA = 256
Q_LORA = 768
D_ATTN = N_HEADS * V_DIM
D_SSM = 512
SSM_GROUP = 16
N_GROUPS = D_SSM // SSM_GROUP
SSM_STATE = 64
N_STATE = N_GROUPS * SSM_STATE
D_FF = 4 * D_MODEL
CHUNK = 64
ROPE_THETA = 10000.0
SOFTMAX_SCALE = (QK_NOPE + QK_ROPE) ** -0.5
LOG2_E = math.log2(math.e)
EPS = 1e-6
NEG_INF = -1e30

LANES = 128
SUBLANES = 8
HEAD_PAD = LANES
D_HEADS_PAD = N_HEADS * HEAD_PAD
COL_CQ = 0
COL_CKV = Q_LORA
COL_U = Q_LORA + KV_LORA
COL_KPE = COL_U + D_SSM
COL_KPE_ROT = COL_KPE + LANES
D_IN_EXT = COL_KPE_ROT + LANES

VMEM_LIMIT = 56 * 1024 * 1024

PROJ_TM = 512
ATTN_TQ = 512
ATTN_TK_SAMPLE = 512
ATTN_SUB = 2
SSM_TT = 128
POST_TM = 512
FF_CHUNK = 1024


def _const_spec(shape):
    nd = len(shape)
    return pl.BlockSpec(shape, lambda *_: (0,) * nd, pipeline_mode=pl.Buffered(1))


def _rms(x, g):
    return x * lax.rsqrt(jnp.mean(x * x, axis=-1, keepdims=True) + EPS) * g


def _prep_kernel(are_ref, aim_ref, ls_ref, bre_ref, bim_ref, lre_ref, lim_ref, bbre_ref, bbim_ref):
    dt = jnp.exp(ls_ref[...])
    lr, li = are_ref[...], aim_ref[...]
    mag = jnp.exp(lr * dt)
    lb_re, lb_im = mag * jnp.cos(li * dt), mag * jnp.sin(li * dt)
    nr, ni = lb_re - 1.0, lb_im
    den = lr * lr + li * li
    coef_re = (nr * lr + ni * li) / den
    coef_im = (ni * lr - nr * li) / den
    lre_ref[...] = lb_re
    lim_ref[...] = lb_im
    bre, bim = bre_ref[...], bim_ref[...]
    cr, ci = coef_re[:, None, :], coef_im[:, None, :]
    bbre_ref[...] = cr * bre - ci * bim
    bbim_ref[...] = cr * bim + ci * bre


def _ssm_prep(a_re, a_im, log_step, b_re, b_im):
    g, n = a_re.shape
    p = b_re.shape[-1]
    bre_t = jnp.swapaxes(b_re, 1, 2)
    bim_t = jnp.swapaxes(b_im, 1, 2)
    return pl.pallas_call(
        _prep_kernel,
        out_shape=(jax.ShapeDtypeStruct((g, n), F32), jax.ShapeDtypeStruct((g, n), F32),
                   jax.ShapeDtypeStruct((g, p, n), F32), jax.ShapeDtypeStruct((g, p, n), F32)),
        name="ssm_prep",
    )(a_re, a_im, log_step.reshape(g, 1), bre_t, bim_t)


def _proj_kernel(x_ref, cq_ref, sqa_ref, sqb_ref, ck_ref, sk_ref, gmix_ref, win_ref, gq_ref, wq_ref, gkv_ref,
                 wk_ref, wvt_ref,
                 q_out, lat_out, kr_out, u_out, *kv_out, v_key_tile):
    xn = _rms(x_ref[...], gmix_ref[...]).astype(BF16)
    proj = jnp.dot(xn, win_ref[...], preferred_element_type=F32)
    u_out[...] = proj[:, COL_U:COL_U + D_SSM]

    cqn = _rms(proj[:, COL_CQ:COL_CQ + Q_LORA], gq_ref[...]).astype(BF16)
    q = jnp.dot(cqn, wq_ref[...], preferred_element_type=F32)
    q_up = pltpu.roll(q, ROPE_HALF, 1)
    q_dn = pltpu.roll(q, D_HEADS_PAD - ROPE_HALF, 1)
    cq_t, sqa_t, sqb_t = cq_ref[...], sqa_ref[...], sqb_ref[...]
    pair_major = v_key_tile is not None

    def put_head(ref, h, tile):
        if pair_major:
            ref[h // 2, :, (h % 2) * HEAD_PAD:(h % 2 + 1) * HEAD_PAD] = tile
        else:
            ref[:, h * HEAD_PAD:(h + 1) * HEAD_PAD] = tile

    for h in range(N_HEADS):
        sl = slice(h * HEAD_PAD, (h + 1) * HEAD_PAD)
        put_head(q_out, h, (q[:, sl] * cq_t + q_up[:, sl] * sqa_t + q_dn[:, sl] * sqb_t).astype(BF16))

    lat = _rms(proj[:, COL_CKV:COL_CKV + KV_LORA], gkv_ref[...])
    lat_out[...] = lat
    kr = (proj[:, COL_KPE:COL_KPE + LANES] * ck_ref[...]
          + proj[:, COL_KPE_ROT:COL_KPE_ROT + LANES] * sk_ref[...])
    kr_out[...] = kr[:, :QK_ROPE]
    if v_key_tile is None:
        return
    kpad_out, vt_out = kv_out
    latb = lat.astype(BF16)
    k_nope = jnp.dot(latb, wk_ref[...], preferred_element_type=F32)
    kr_placed = pltpu.roll(kr, QK_NOPE, 1)
    for h in range(N_HEADS):
        put_head(kpad_out, h, (k_nope[:, h * HEAD_PAD:(h + 1) * HEAD_PAD] + kr_placed).astype(BF16))
    vt = lax.dot_general(wvt_ref[...], latb, (((1,), (1,)), ((), ())),
                         preferred_element_type=F32)
    ones = lax.broadcasted_iota(jnp.int32, vt.shape, 0) % HEAD_PAD >= V_DIM
    vt = jnp.where(ones, 1.0, vt).astype(BF16)
    for s in range(vt.shape[1] // v_key_tile):
        vt_out[s] = vt[:, s * v_key_tile:(s + 1) * v_key_tile]


def _proj(x2d, tabs, w, tm, stream_len, v_key_tile):
    n = x2d.shape[0]
    cosq, sinq_up, sinq_dn, cosk, sink = tabs
    n_tab = cosq.shape[0] // tm
    row = lambda i: (i, 0)
    tab = lambda i: (i % n_tab, 0)
    rest_shape = [
        jax.ShapeDtypeStruct((n, KV_LORA), F32),
        jax.ShapeDtypeStruct((n, QK_ROPE), F32),
        jax.ShapeDtypeStruct((n, D_SSM), F32),
    ]
    rest_specs = [pl.BlockSpec((tm, KV_LORA), row), pl.BlockSpec((tm, QK_ROPE), row), pl.BlockSpec((tm, D_SSM), row)]
    if v_key_tile is None:
        out_shape = [jax.ShapeDtypeStruct((n, D_HEADS_PAD), BF16)] + rest_shape
        out_specs = [pl.BlockSpec((tm, D_HEADS_PAD), row)] + rest_specs
    else:
        assert stream_len % tm == 0 and tm % v_key_tile == 0
        tiles_per_stream = stream_len // tm
        n_streams = n // stream_len
        pair_shape = jax.ShapeDtypeStruct((n_streams, N_HEADS // 2, stream_len, 2 * HEAD_PAD), BF16)
        pair_spec = pl.BlockSpec((None, N_HEADS // 2, tm, 2 * HEAD_PAD),
                                 lambda i: (i // tiles_per_stream, 0, i % tiles_per_stream, 0))
        out_shape = [pair_shape] + rest_shape + [
            pair_shape,
            jax.ShapeDtypeStruct((n_streams, stream_len // v_key_tile, D_HEADS_PAD, v_key_tile), BF16)]
        out_specs = [pair_spec] + rest_specs + [
            pair_spec,
            pl.BlockSpec((None, tm // v_key_tile, D_HEADS_PAD, v_key_tile),
                         lambda i: (i // tiles_per_stream, i % tiles_per_stream, 0, 0))]
    return pl.pallas_call(
        functools.partial(_proj_kernel, v_key_tile=v_key_tile),
        grid=(n // tm,),
        in_specs=[
            pl.BlockSpec((tm, D_MODEL), row),
            pl.BlockSpec((tm, LANES), tab), pl.BlockSpec((tm, LANES), tab), pl.BlockSpec((tm, LANES), tab),
            pl.BlockSpec((tm, LANES), tab), pl.BlockSpec((tm, LANES), tab),
            _const_spec((1, D_MODEL)), _const_spec((D_MODEL, D_IN_EXT)),
            _const_spec((1, Q_LORA)), _const_spec((Q_LORA, D_HEADS_PAD)),
            _const_spec((1, KV_LORA)), _const_spec((KV_LORA, D_HEADS_PAD)),
            _const_spec((D_HEADS_PAD, KV_LORA)),
        ],
        out_specs=tuple(out_specs),
        out_shape=tuple(out_shape),
        compiler_params=pltpu.CompilerParams(dimension_semantics=("parallel",),
                                             vmem_limit_bytes=VMEM_LIMIT),
        name="proj",
    )(x2d, cosq, sinq_up, sinq_dn, cosk, sink, w["g_mix"], w["w_in_ext"], w["g_q_a"], w["w_q_pad"], w["g_kv_a"],
      w["w_k_pad"], w["w_v_t"])


def _attn_prompt_kernel(q_ref, k_ref, vt_ref, o_ref, s00, s01, s10, s11, mx_s, *, tq):
    i = pl.program_id(2)
    key_chunk = lax.broadcasted_iota(jnp.int32, (tq, tq), 0) // CHUNK
    query_chunk = lax.broadcasted_iota(jnp.int32, (tq, tq), 1) // CHUNK
    diag_mask = key_chunk <= query_chunk
    heads = [slice(h * HEAD_PAD, (h + 1) * HEAD_PAD) for h in range(2)]
    slots = ((s00, s01), (s10, s11))
    n_sub = ATTN_SUB
    ts = tq // n_sub

    def scores(j, slot):
        off = pl.multiple_of(j * tq, tq)
        for h, (sl, buf) in enumerate(zip(heads, slots[slot])):
            st = lax.dot_general(k_ref[pl.ds(off, tq), sl], q_ref[:, sl], (((1,), (1,)), ((), ())),
                                 preferred_element_type=F32)
            buf[...] = st
            for s in range(n_sub):
                mx_s[slot, h, s] = jnp.max(st[s * ts:(s + 1) * ts], axis=0, keepdims=True)

    def consume(j, slot, state, mask):
        for s in range(n_sub):
            rows = slice(s * ts, (s + 1) * ts)
            lo = s * ts if mask is not None else 0
            new_state = []
            for h, (sl, buf, (m, acc)) in enumerate(zip(heads, slots[slot], state)):
                st = buf[rows, lo:]
                if mask is not None:
                    st = jnp.where(mask[rows, lo:], st, NEG_INF)
                    m_new = jnp.maximum(m[:, lo:], jnp.max(st, axis=0, keepdims=True))
                else:
                    m_new = jnp.maximum(m, mx_s[slot, h, s])
                alpha = jnp.exp2(m[:, lo:] - m_new)
                p = jnp.exp2(st - m_new).astype(BF16)
                acc_new = acc[:, lo:] * alpha + jnp.dot(vt_ref[j, sl, rows], p, preferred_element_type=F32)
                if lo:
                    m_new = jnp.concatenate([m[:, :lo], m_new], axis=1)
                    acc_new = jnp.concatenate([acc[:, :lo], acc_new], axis=1)
                new_state.append((m_new, acc_new))
            state = tuple(new_state)
        return state

    def pair(p, state):
        scores(2 * p + 1, 1)
        state = consume(2 * p, 0, state, None)
        scores(2 * p + 2, 0)
        return consume(2 * p + 1, 1, state, None)

    init = tuple((jnp.full((1, tq), NEG_INF, F32), jnp.zeros((HEAD_PAD, tq), F32)) for _ in heads)
    scores(0, 0)
    state = lax.fori_loop(0, i // 2, pair, init)

    def even_tail(state):
        return consume(i, 0, state, diag_mask)

    def odd_tail(state):
        scores(i, 1)
        state = consume(i - 1, 0, state, None)
        return consume(i, 1, state, diag_mask)

    state = lax.cond(i % 2 == 0, even_tail, odd_tail, state)
    o_t = jnp.concatenate([acc[:V_DIM] / acc[V_DIM:V_DIM + 1] for _, acc in state], axis=0)
    o_ref[...] = o_t.T


def _attn_prompt(q, kpad, vt, tq):
    b, _, t, _ = q.shape
    assert tq % CHUNK == 0 and t % tq == 0 and vt.shape == (b, t // tq, D_HEADS_PAD, tq)
    return pl.pallas_call(
        functools.partial(_attn_prompt_kernel, tq=tq),
        grid=(b, N_HEADS // 2, t // tq),
        in_specs=[pl.BlockSpec((None, None, tq, 2 * HEAD_PAD), lambda bi, hp, i: (bi, hp, i, 0)),
                  pl.BlockSpec((None, None, t, 2 * HEAD_PAD), lambda bi, hp, i: (bi, hp, 0, 0)),
                  pl.BlockSpec((None, t // tq, 2 * HEAD_PAD, tq), lambda bi, hp, i: (bi, 0, hp, 0))],
        out_specs=pl.BlockSpec((None, tq, 2 * V_DIM), lambda bi, hp, i: (bi, i, hp)),
        out_shape=jax.ShapeDtypeStruct((b, t, D_ATTN), F32),
        scratch_shapes=[pltpu.VMEM((tq, tq), F32)] * 4 + [pltpu.VMEM((2, 2, ATTN_SUB, 1, tq), F32)],
        compiler_params=pltpu.CompilerParams(
            dimension_semantics=("parallel", "parallel", "arbitrary"), vmem_limit_bytes=VMEM_LIMIT),
        name="attn_prompt",
    )(q, kpad, vt)


D_KCAT = KV_LORA + LANES


def _attn_sample_kernel(q_ref, plat_ref, pkr_ref, nlat_ref, nkr_ref, wabs_ref, wv_ref, o_ref, s0, s1, sn, *, tk):
    tq = q_ref.shape[0]
    n_past = plat_ref.shape[0] // tk
    qcat = jnp.concatenate(
        [jnp.dot(q_ref[:, h * HEAD_PAD:(h + 1) * HEAD_PAD], wabs_ref[h], preferred_element_type=F32)
         for h in range(N_HEADS)], axis=0).astype(BF16)
    tiles = [(plat_ref.at[pl.ds(j * tk, tk)], pkr_ref.at[pl.ds(j * tk, tk)], (s0, s1)[j % 2])
             for j in range(n_past)] + [(nlat_ref, nkr_ref, sn)]

    def scores(lat_ref, kr_ref, buf):
        kr = kr_ref[...].astype(BF16)
        kcat = jnp.concatenate([lat_ref[...].astype(BF16), kr,
                                jnp.zeros((kr.shape[0], LANES - QK_ROPE), BF16)], axis=1)
        buf[...] = lax.dot_general(qcat, kcat, (((1,), (1,)), ((), ())), preferred_element_type=F32)

    def consume(lat_ref, buf, carry):
        keys = buf.shape[1]
        ts = min(keys, tk // ATTN_SUB)
        for c in range(keys // ts):
            m, l, acc = carry
            s = buf[:, c * ts:(c + 1) * ts]
            m_new = jnp.maximum(m, jnp.max(s, axis=-1, keepdims=True))
            alpha = jnp.exp2(m - m_new)
            p = jnp.exp2(s - m_new)
            l = l * alpha + jnp.sum(p, axis=-1, keepdims=True)
            acc = acc * alpha + jnp.dot(p.astype(BF16), lat_ref[c * ts:(c + 1) * ts, :].astype(BF16),
                                        preferred_element_type=F32)
            carry = (m_new, l, acc)
        return carry

    rows = N_HEADS * tq
    carry = (jnp.full((rows, 1), NEG_INF, F32), jnp.zeros((rows, 1), F32), jnp.zeros((rows, KV_LORA), F32))
    scores(*tiles[0])
    for idx, (lat_ref, _, buf) in enumerate(tiles):
        if idx + 1 < len(tiles):
            scores(*tiles[idx + 1])
        carry = consume(lat_ref, buf, carry)
    _, l, acc = carry
    o_lat = (acc / l).astype(BF16)
    out = jnp.zeros((tq, D_ATTN), F32)
    for h in range(N_HEADS):
        out = out + jnp.dot(o_lat[h * tq:(h + 1) * tq], wv_ref[h], preferred_element_type=F32)
    o_ref[...] = out


def _attn_sample(q, past_lat, past_kr, new_lat, new_kr, w, tk):
    b, tq, _ = q.shape
    past = past_lat.shape[1]
    assert past % tk == 0
    blk = lambda bi: (bi, 0, 0)
    return pl.pallas_call(
        functools.partial(_attn_sample_kernel, tk=tk),
        grid=(b,),
        in_specs=[pl.BlockSpec((None, tq, D_HEADS_PAD), blk),
                  pl.BlockSpec((None, past, KV_LORA), blk), pl.BlockSpec((None, past, QK_ROPE), blk),
                  pl.BlockSpec((None, tq, KV_LORA), blk), pl.BlockSpec((None, tq, QK_ROPE), blk),
                  _const_spec((N_HEADS, HEAD_PAD, D_KCAT)), _const_spec((N_HEADS, KV_LORA, D_ATTN))],
        out_specs=pl.BlockSpec((None, tq, D_ATTN), blk),
        out_shape=jax.ShapeDtypeStruct((b, tq, D_ATTN), F32),
        scratch_shapes=[pltpu.VMEM((N_HEADS * tq, tk), F32), pltpu.VMEM((N_HEADS * tq, tk), F32),
                        pltpu.VMEM((N_HEADS * tq, tq), F32)],
        compiler_params=pltpu.CompilerParams(dimension_semantics=("parallel",),
                                             vmem_limit_bytes=VMEM_LIMIT),
        name="attn_sample",
    )(q, past_lat, past_kr, new_lat, new_kr, w["w_q_abs"], w["w_v_heads"])


SSM_CH = LANES
SSM_ST = SSM_CH // SSM_GROUP * SSM_STATE
N_SSM_CHUNKS = D_SSM // SSM_CH


def _ssm_kernel(u_ref, h0_ref, lre_ref, lim_ref, bc_ref, cc_ref, d_ref, wglu_ref, gssm_ref,
                y_ref, hout_ref, x_s, h_s, hc_s, *, tt):
    ti = pl.program_id(1)

    @pl.when(ti == 0)
    def _():
        hc_s[...] = h0_ref[...]

    rows = tt * SUBLANES
    u = jnp.swapaxes(u_ref[...], 0, 1).reshape(rows, D_SSM)
    ub = u.astype(BF16)
    chunks = []
    for c in range(N_SSM_CHUNKS):
        re = slice(c * SSM_ST, (c + 1) * SSM_ST)
        im = slice(N_STATE + c * SSM_ST, N_STATE + (c + 1) * SSM_ST)
        chunks.append((re, im))
        xc = jnp.dot(ub[:, c * SSM_CH:(c + 1) * SSM_CH], bc_ref[c], preferred_element_type=F32)
        x_s[:, :, re] = xc[:, :SSM_ST].reshape(tt, SUBLANES, SSM_ST)
        x_s[:, :, im] = xc[:, SSM_ST:].reshape(tt, SUBLANES, SSM_ST)

    for re, im in chunks:
        lr = jnp.broadcast_to(lre_ref[:, re], (SUBLANES, SSM_ST))
        li = jnp.broadcast_to(lim_ref[:, re], (SUBLANES, SSM_ST))

        def step(t, carry, re=re, im=im, lr=lr, li=li):
            hr, hi = carry
            nhr = lr * hr - li * hi + x_s[t, :, re]
            nhi = lr * hi + li * hr + x_s[t, :, im]
            h_s[t, :, re] = nhr
            h_s[t, :, im] = nhi
            return nhr, nhi

        hr, hi = lax.fori_loop(0, tt, step, (hc_s[:, re], hc_s[:, im]), unroll=True)
        hc_s[:, re] = hr
        hc_s[:, im] = hi

    ys = []
    for c, (re, im) in enumerate(chunks):
        hcat = jnp.concatenate([h_s[:, :, re], h_s[:, :, im]], axis=-1).reshape(rows, 2 * SSM_ST)
        ys.append(jnp.dot(hcat.astype(BF16), cc_ref[c], preferred_element_type=F32))
    y = jnp.concatenate(ys, axis=1) + d_ref[...] * u
    y = jax.nn.gelu(y, approximate=True)
    z = jnp.dot(y.astype(BF16), wglu_ref[...], preferred_element_type=F32)
    y = y * (1.0 / (1.0 + jnp.exp(-z)))
    y_ref[...] = jnp.swapaxes(_rms(y, gssm_ref[...]).reshape(tt, SUBLANES, D_SSM), 0, 1)

    @pl.when(ti == pl.num_programs(1) - 1)
    def _():
        hout_ref[...] = hc_s[...]


def _ssm(u, h0, w, tt):
    nb, t, _ = u.shape
    assert nb % SUBLANES == 0 and t % tt == 0
    return pl.pallas_call(
        functools.partial(_ssm_kernel, tt=tt),
        grid=(nb // SUBLANES, t // tt),
        in_specs=[pl.BlockSpec((SUBLANES, tt, D_SSM), lambda bg, ti: (bg, ti, 0)),
                  pl.BlockSpec((SUBLANES, 2 * N_STATE), lambda bg, ti: (bg, 0)),
                  _const_spec((1, N_STATE)), _const_spec((1, N_STATE)),
                  _const_spec((N_SSM_CHUNKS, SSM_CH, 2 * SSM_ST)), _const_spec((N_SSM_CHUNKS, 2 * SSM_ST, SSM_CH)),
                  _const_spec((1, D_SSM)), _const_spec((D_SSM, D_SSM)), _const_spec((1, D_SSM))],
        out_specs=(pl.BlockSpec((SUBLANES, tt, D_SSM), lambda bg, ti: (bg, ti, 0)),
                   pl.BlockSpec((SUBLANES, 2 * N_STATE), lambda bg, ti: (bg, 0))),
        out_shape=(jax.ShapeDtypeStruct((nb, t, D_SSM), F32),
                   jax.ShapeDtypeStruct((nb, 2 * N_STATE), F32)),
        scratch_shapes=[pltpu.VMEM((tt, SUBLANES, 2 * N_STATE), F32),
                        pltpu.VMEM((tt, SUBLANES, 2 * N_STATE), F32),
                        pltpu.VMEM((SUBLANES, 2 * N_STATE), F32)],
        compiler_params=pltpu.CompilerParams(dimension_semantics=("parallel", "arbitrary"),
                                             vmem_limit_bytes=VMEM_LIMIT),
        name="ssm",
    )(u, h0, w["lam_re"], w["lam_im"], w["b_chunks"], w["c_chunks"], w["d_skip"], w["w_glu"],
      w["g_ssm_out"])


def _post_kernel(x_ref, attn_ref, ssm_ref, gattn_ref, wouta_ref, wouts_ref, gmlp_ref, wup_ref, wdown_ref,
                 gfin_ref, y_ref):
    an = _rms(attn_ref[...], gattn_ref[...]).astype(BF16)
    mixed = (jnp.dot(an, wouta_ref[...], preferred_element_type=F32)
             + jnp.dot(ssm_ref[...].astype(BF16), wouts_ref[...], preferred_element_type=F32))
    h = x_ref[...] + mixed
    hn = _rms(h, gmlp_ref[...]).astype(BF16)
    acc = jnp.zeros(h.shape, F32)
    for c in range(D_FF // FF_CHUNK):
        ff = slice(c * FF_CHUNK, (c + 1) * FF_CHUNK)
        a = jnp.dot(hn, wup_ref[:, ff], preferred_element_type=F32)
        a = jnp.square(jnp.maximum(a, 0.0))
        acc = acc + jnp.dot(a.astype(BF16), wdown_ref[ff, :], preferred_element_type=F32)
    y_ref[...] = _rms(h + acc, gfin_ref[...])


def _post(x2d, attn2d, ssm2d, w, tm):
    n = x2d.shape[0]
    row = lambda i: (i, 0)
    return pl.pallas_call(
        _post_kernel,
        grid=(n // tm,),
        in_specs=[pl.BlockSpec((tm, D_MODEL), row), pl.BlockSpec((tm, D_ATTN), row),
                  pl.BlockSpec((tm, D_SSM), row),
                  _const_spec((1, D_ATTN)), _const_spec((D_ATTN, D_MODEL)), _const_spec((D_SSM, D_MODEL)),
                  _const_spec((1, D_MODEL)), _const_spec((D_MODEL, D_FF)), _const_spec((D_FF, D_MODEL)),
                  _const_spec((1, D_MODEL))],
        out_specs=pl.BlockSpec((tm, D_MODEL), row),
        out_shape=jax.ShapeDtypeStruct((n, D_MODEL), F32),
        compiler_params=pltpu.CompilerParams(dimension_semantics=("parallel",),
                                             vmem_limit_bytes=VMEM_LIMIT),
        name="post",
    )(x2d, attn2d, ssm2d, w["g_attn_out"], w["w_out_attn"], w["w_out_ssm"], w["g_mlp"], w["w_up"],
      w["w_down"], w["g_final"])


def _rope_tables(pos):
    t = pos.shape[0]
    inv_freq = ROPE_THETA ** (-(jnp.arange(ROPE_HALF, dtype=F32) * 2.0) / QK_ROPE)
    ang = pos.astype(F32)[:, None] * inv_freq[None, :]
    cc = jnp.tile(jnp.cos(ang), (1, 2))
    ss = jnp.tile(jnp.sin(ang), (1, 2))
    pad_hi = jnp.zeros((t, HEAD_PAD - QK_NOPE - QK_ROPE), F32)
    qs = SOFTMAX_SCALE * LOG2_E
    cosq = jnp.concatenate([jnp.full((t, QK_NOPE), qs, F32), cc * qs, pad_hi], axis=1)
    z_half = jnp.zeros((t, ROPE_HALF), F32)
    sin_h = jnp.sin(ang) * qs
    sinq_up = jnp.concatenate([jnp.zeros((t, QK_NOPE), F32), z_half, sin_h, pad_hi], axis=1)
    sinq_dn = jnp.concatenate([jnp.zeros((t, QK_NOPE), F32), -sin_h, z_half, pad_hi], axis=1)
    pad_k = jnp.zeros((t, LANES - QK_ROPE), F32)
    cosk = jnp.concatenate([cc, pad_k], axis=1)
    sink = jnp.concatenate([ss, pad_k], axis=1)
    return cosq, sinq_up, sinq_dn, cosk, sink


def _rot_cols(w_x1, w_x2):
    return -w_x2, w_x1


def _layer_weights(g_mix, w_in, g_q_a, w_q_up, g_kv_a, w_kv_up, a_re, a_im, log_step, b_re, b_im,
                   c_re, c_im, d_skip, w_glu, g_attn_out, g_ssm_out, w_out, g_mlp, w_up, w_down, g_final):
    w = {}
    w["g_mix"] = g_mix[None, :]
    w_cq, w_ckv = w_in[:, :Q_LORA], w_in[:, Q_LORA:Q_LORA + KV_LORA]
    w_kpe = w_in[:, Q_LORA + KV_LORA:Q_LORA + KV_LORA + QK_ROPE]
    w_u = w_in[:, Q_LORA + KV_LORA + QK_ROPE:]
    rot1, rot2 = _rot_cols(w_kpe[:, :ROPE_HALF], w_kpe[:, ROPE_HALF:])
    zk = jnp.zeros((D_MODEL, LANES - QK_ROPE), F32)
    w["w_in_ext"] = jnp.concatenate([w_cq, w_ckv, w_u, w_kpe, zk, rot1, rot2, zk], axis=1).astype(BF16)

    w["g_q_a"] = g_q_a[None, :]
    wq = w_q_up.reshape(Q_LORA, N_HEADS, QK_NOPE + QK_ROPE)
    nope, r1, r2 = wq[:, :, :QK_NOPE], wq[:, :, QK_NOPE:QK_NOPE + ROPE_HALF], wq[:, :, QK_NOPE + ROPE_HALF:]
    zq = jnp.zeros((Q_LORA, N_HEADS, HEAD_PAD - QK_NOPE - QK_ROPE), F32)
    w["w_q_pad"] = jnp.concatenate([nope, r1, r2, zq], axis=2).reshape(Q_LORA, D_HEADS_PAD).astype(BF16)

    w["g_kv_a"] = g_kv_a[None, :]
    wkv = w_kv_up.reshape(KV_LORA, N_HEADS, QK_NOPE + V_DIM)
    zkv = jnp.zeros((KV_LORA, N_HEADS, HEAD_PAD - QK_NOPE), F32)
    wk_pad = jnp.concatenate([wkv[:, :, :QK_NOPE], zkv], axis=2).reshape(KV_LORA, D_HEADS_PAD)
    wv_pad = jnp.concatenate([wkv[:, :, QK_NOPE:], zkv], axis=2).reshape(KV_LORA, D_HEADS_PAD)
    w["w_k_pad"] = wk_pad.astype(BF16)
    w["w_v_t"] = wv_pad.T.astype(BF16)
    wk_t = jnp.transpose(wkv[:, :, :QK_NOPE], (1, 2, 0))
    q_abs = jnp.zeros((N_HEADS, HEAD_PAD, D_KCAT), F32).at[:, :QK_NOPE, :KV_LORA].set(wk_t)
    q_abs = q_abs.at[:, QK_NOPE + jnp.arange(QK_ROPE), KV_LORA + jnp.arange(QK_ROPE)].set(1.0)
    w["w_q_abs"] = q_abs.astype(BF16)
    wv_h = jnp.transpose(wkv[:, :, QK_NOPE:], (1, 0, 2))
    w["w_v_heads"] = jnp.einsum("hce,hg->hcge", wv_h, jnp.eye(N_HEADS, dtype=F32)).reshape(
        N_HEADS, KV_LORA, D_ATTN).astype(BF16)

    lam_re, lam_im, bb_re, bb_im = _ssm_prep(a_re, a_im, log_step, b_re, b_im)
    w["lam_re"] = lam_re.reshape(1, N_STATE)
    w["lam_im"] = lam_im.reshape(1, N_STATE)
    gpc = SSM_CH // SSM_GROUP
    eye = jnp.eye(gpc, dtype=F32)
    by_chunk = lambda m: m.reshape(N_SSM_CHUNKS, gpc, SSM_GROUP, SSM_STATE)
    blk = lambda m: jnp.einsum("cgpn,gh->cgphn", by_chunk(m), eye).reshape(N_SSM_CHUNKS, SSM_CH, SSM_ST)
    w["b_chunks"] = jnp.concatenate([blk(bb_re), blk(bb_im)], axis=2).astype(BF16)
    blk_t = lambda m: jnp.einsum("cgpn,gh->cgnhp", by_chunk(m), eye).reshape(N_SSM_CHUNKS, SSM_ST, SSM_CH)
    w["c_chunks"] = jnp.concatenate([blk_t(c_re), blk_t(-c_im)], axis=1).astype(BF16)
    w["d_skip"] = d_skip[None, :]
    w["w_glu"] = w_glu.astype(BF16)
    w["g_ssm_out"] = g_ssm_out[None, :]

    w["g_attn_out"] = g_attn_out[None, :]
    w["w_out_attn"] = w_out[:D_ATTN].astype(BF16)
    w["w_out_ssm"] = w_out[D_ATTN:].astype(BF16)
    w["g_mlp"] = g_mlp[None, :]
    w["w_up"] = w_up.astype(BF16)
    w["w_down"] = w_down.astype(BF16)
    w["g_final"] = g_final[None, :]
    return w


def _pack_state(h_re, h_im):
    nb = h_re.shape[0]
    return jnp.concatenate([h_re.reshape(nb, N_STATE), h_im.reshape(nb, N_STATE)], axis=1)


def _unpack_state(h):
    nb = h.shape[0]
    return (h[:, :N_STATE].reshape(nb, N_GROUPS, SSM_STATE), h[:, N_STATE:].reshape(nb, N_GROUPS, SSM_STATE))


def _branch(x, pos, past, h0, w, *, proj_tm, post_tm, ssm_tt):
    b, t, _ = x.shape
    n = b * t
    x2d = x.reshape(n, D_MODEL)
    tabs = _rope_tables(pos)
    if t < proj_tm:
        assert proj_tm % t == 0
        tabs = tuple(jnp.tile(a, (proj_tm // t, 1)) for a in tabs)
    else:
        assert t % proj_tm == 0
    if past is None:
        tq = min(ATTN_TQ, t)
        q, lat, kr, u, kpad, vt = _proj(x2d, tabs, w, proj_tm, t, tq)
        attn = _attn_prompt(q, kpad, vt, tq)
    else:
        q, lat, kr, u = _proj(x2d, tabs, w, proj_tm, t, None)
        past_lat, past_kr = past
        plen = past_lat.shape[1]
        assert plen % CHUNK == 0 and t <= CHUNK
        attn = _attn_sample(q.reshape(b, t, D_HEADS_PAD), past_lat, past_kr, lat.reshape(b, t, KV_LORA),
                            kr.reshape(b, t, QK_ROPE), w, min(ATTN_TK_SAMPLE, plen))
    y_ssm, h_fin = _ssm(u.reshape(b, t, D_SSM), h0, w, min(ssm_tt, t))
    ssm2d = y_ssm.reshape(n, D_SSM)
    y = _post(x2d, attn.reshape(n, D_ATTN), ssm2d, w, post_tm)
    h_re, h_im = _unpack_state(h_fin)
    return (y.reshape(b, t, D_MODEL), lat.reshape(1, b, t, KV_LORA), kr.reshape(1, b, t, QK_ROPE),
            h_re[None], h_im[None])


def kernel(x_prompt, x_sample, cache_kv_latent, cache_k_rope, state_ssm_re, state_ssm_im, g_mix, w_in, g_q_a,
           w_q_up, g_kv_a, w_kv_up, a_re, a_im, log_step, b_re, b_im, c_re, c_im, d_skip, w_glu, g_attn_out,
           g_ssm_out, w_out, g_mlp, w_up, w_down, g_final):
    assert g_mix.shape[0] == 1, "single-layer trunk"
    w = _layer_weights(g_mix[0], w_in[0], g_q_a[0], w_q_up[0], g_kv_a[0], w_kv_up[0], a_re[0], a_im[0],
                       log_step[0], b_re[0], b_im[0], c_re[0], c_im[0], d_skip[0], w_glu[0], g_attn_out[0],
                       g_ssm_out[0], w_out[0], g_mlp[0], w_up[0], w_down[0], g_final)
    tiles = dict(proj_tm=PROJ_TM, post_tm=POST_TM, ssm_tt=SSM_TT)

    bp, tp, _ = x_prompt.shape
    pos_p = jnp.arange(tp, dtype=jnp.int32)
    h0p = jnp.zeros((bp, 2 * N_STATE), F32)
    y_p, lat_p, kr_p, hr_p, hi_p = _branch(x_prompt, pos_p, None, h0p, w, **tiles)

    bs, ts, _ = x_sample.shape
    plen = cache_kv_latent.shape[2]
    pos_s = plen + jnp.arange(ts, dtype=jnp.int32)
    h0s = _pack_state(state_ssm_re[0], state_ssm_im[0])
    y_s, lat_s, kr_s, hr_s, hi_s = _branch(x_sample, pos_s, (cache_kv_latent[0], cache_k_rope[0]), h0s, w, **tiles)
    return (y_p, y_s, lat_p, kr_p, hr_p, hi_p, lat_s, kr_s, hr_s, hi_s)
```

```python
import functools
import math

import jax
import jax.numpy as jnp
from jax import lax
from jax.experimental import pallas as pl
from jax.experimental.pallas import tpu as pltpu

F32 = jnp.float32
BF16 = jnp.bfloat16

D_MODEL = 1024
N_HEADS = 8
QK_NOPE = 64
QK_ROPE = 32
ROPE_HALF = QK_ROPE // 2
V_DIM = 64
KV_LORA = 256
Q_LORA = 768
D_ATTN = N_HEADS * V_DIM
D_SSM = 512
SSM_GROUP = 16
N_GROUPS = D_SSM // SSM_GROUP
SSM_STATE = 64
N_STATE = N_GROUPS * SSM_STATE
D_FF = 4 * D_MODEL
CHUNK = 64
ROPE_THETA = 10000.0
SOFTMAX_SCALE = (QK_NOPE + QK_ROPE) ** -0.5
LOG2_E = math.log2(math.e)
EPS = 1e-6
NEG_INF = -1e30

LANES = 128
SUBLANES = 8
HEAD_PAD = LANES
D_HEADS_PAD = N_HEADS * HEAD_PAD
COL_CQ = 0
COL_CKV = Q_LORA
COL_U = Q_LORA + KV_LORA
COL_KPE = COL_U + D_SSM
COL_KPE_ROT = COL_KPE + LANES
D_IN_EXT = COL_KPE_ROT + LANES

VMEM_LIMIT = 56 * 1024 * 1024

PROJ_TM = 1024
ATTN_TQ = 512
ATTN_TK_SAMPLE = 512
ATTN_SUB = 2
SSM_TT = 128
POST_TM = 512
FF_CHUNK = 1024


def _const_spec(shape):
    nd = len(shape)
    return pl.BlockSpec(shape, lambda *_: (0,) * nd, pipeline_mode=pl.Buffered(1))


def _rms(x, g):
    return x * lax.rsqrt(jnp.mean(x * x, axis=-1, keepdims=True) + EPS) * g


def _prep_kernel(are_ref, aim_ref, ls_ref, bre_ref, bim_ref, lre_ref, lim_ref, bbre_ref, bbim_ref):
    dt = jnp.exp(ls_ref[...])
    lr, li = are_ref[...], aim_ref[...]
    mag = jnp.exp(lr * dt)
    lb_re, lb_im = mag * jnp.cos(li * dt), mag * jnp.sin(li * dt)
    nr, ni = lb_re - 1.0, lb_im
    den = lr * lr + li * li
    coef_re = (nr * lr + ni * li) / den
    coef_im = (ni * lr - nr * li) / den
    lre_ref[...] = lb_re
    lim_ref[...] = lb_im
    bre, bim = bre_ref[...], bim_ref[...]
    cr, ci = coef_re[:, None, :], coef_im[:, None, :]
    bbre_ref[...] = cr * bre - ci * bim
    bbim_ref[...] = cr * bim + ci * bre


def _ssm_prep(a_re, a_im, log_step, b_re, b_im):
    g, n = a_re.shape
    p = b_re.shape[-1]
    bre_t = jnp.swapaxes(b_re, 1, 2)
    bim_t = jnp.swapaxes(b_im, 1, 2)
    return pl.pallas_call(
        _prep_kernel,
        out_shape=(jax.ShapeDtypeStruct((g, n), F32), jax.ShapeDtypeStruct((g, n), F32),
                   jax.ShapeDtypeStruct((g, p, n), F32), jax.ShapeDtypeStruct((g, p, n), F32)),
        name="ssm_prep",
    )(a_re, a_im, log_step.reshape(g, 1), bre_t, bim_t)


def _proj_kernel(x_ref, cq_ref, sqa_ref, sqb_ref, ck_ref, sk_ref, gmix_ref, win_ref, gq_ref, wq_ref, gkv_ref,
                 wk_ref, wvt_ref,
                 q_out, lat_out, kr_out, u_out, *kv_out, v_key_tile):
    xn = _rms(x_ref[...], gmix_ref[...]).astype(BF16)
    proj = jnp.dot(xn, win_ref[...], preferred_element_type=F32)
    u_out[...] = proj[:, COL_U:COL_U + D_SSM]

    cqn = _rms(proj[:, COL_CQ:COL_CQ + Q_LORA], gq_ref[...]).astype(BF16)
    q = jnp.dot(cqn, wq_ref[...], preferred_element_type=F32)
    q_up = pltpu.roll(q, ROPE_HALF, 1)
    q_dn = pltpu.roll(q, D_HEADS_PAD - ROPE_HALF, 1)
    cq_t, sqa_t, sqb_t = cq_ref[...], sqa_ref[...], sqb_ref[...]
    pair_major = v_key_tile is not None

    def put_head(ref, h, tile):
        if pair_major:
            ref[h // 2, :, (h % 2) * HEAD_PAD:(h % 2 + 1) * HEAD_PAD] = tile
        else:
            ref[:, h * HEAD_PAD:(h + 1) * HEAD_PAD] = tile

    for h in range(N_HEADS):
        sl = slice(h * HEAD_PAD, (h + 1) * HEAD_PAD)
        put_head(q_out, h, (q[:, sl] * cq_t + q_up[:, sl] * sqa_t + q_dn[:, sl] * sqb_t).astype(BF16))

    lat = _rms(proj[:, COL_CKV:COL_CKV + KV_LORA], gkv_ref[...])
    lat_out[...] = lat
    kr = (proj[:, COL_KPE:COL_KPE + LANES] * ck_ref[...]
          + proj[:, COL_KPE_ROT:COL_KPE_ROT + LANES] * sk_ref[...])
    kr_out[...] = kr[:, :QK_ROPE]
    if v_key_tile is None:
        return
    kpad_out, vt_out = kv_out
    latb = lat.astype(BF16)
    k_nope = jnp.dot(latb, wk_ref[...], preferred_element_type=F32)
    kr_placed = pltpu.roll(kr, QK_NOPE, 1)
    for h in range(N_HEADS):
        put_head(kpad_out, h, (k_nope[:, h * HEAD_PAD:(h + 1) * HEAD_PAD] + kr_placed).astype(BF16))
    vt = lax.dot_general(wvt_ref[...], latb, (((1,), (1,)), ((), ())),
                         preferred_element_type=F32)
    ones = lax.broadcasted_iota(jnp.int32, vt.shape, 0) % HEAD_PAD >= V_DIM
    vt = jnp.where(ones, 1.0, vt).astype(BF16)
    for s in range(vt.shape[1] // v_key_tile):
        vt_out[s] = vt[:, s * v_key_tile:(s + 1) * v_key_tile]


def _proj(x2d, tabs, w, tm, stream_len, v_key_tile):
    n = x2d.shape[0]
    cosq, sinq_up, sinq_dn, cosk, sink = tabs
    n_tab = cosq.shape[0] // tm
    row = lambda i: (i, 0)
    tab = lambda i: (i % n_tab, 0)
    rest_shape = [
        jax.ShapeDtypeStruct((n, KV_LORA), F32),
        jax.ShapeDtypeStruct((n, QK_ROPE), F32),
        jax.ShapeDtypeStruct((n, D_SSM), F32),
    ]
    rest_specs = [pl.BlockSpec((tm, KV_LORA), row), pl.BlockSpec((tm, QK_ROPE), row), pl.BlockSpec((tm, D_SSM), row)]
    if v_key_tile is None:
        out_shape = [jax.ShapeDtypeStruct((n, D_HEADS_PAD), BF16)] + rest_shape
        out_specs = [pl.BlockSpec((tm, D_HEADS_PAD), row)] + rest_specs
    else:
        assert stream_len % tm == 0 and tm % v_key_tile == 0
        tiles_per_stream = stream_len // tm
        n_streams = n // stream_len
        pair_shape = jax.ShapeDtypeStruct((n_streams, N_HEADS // 2, stream_len, 2 * HEAD_PAD), BF16)
        pair_spec = pl.BlockSpec((None, N_HEADS // 2, tm, 2 * HEAD_PAD),
                                 lambda i: (i // tiles_per_stream, 0, i % tiles_per_stream, 0))
        out_shape = [pair_shape] + rest_shape + [
            pair_shape,
            jax.ShapeDtypeStruct((n_streams, stream_len // v_key_tile, D_HEADS_PAD, v_key_tile), BF16)]
        out_specs = [pair_spec] + rest_specs + [
            pair_spec,
            pl.BlockSpec((None, tm // v_key_tile, D_HEADS_PAD, v_key_tile),
                         lambda i: (i // tiles_per_stream, i % tiles_per_stream, 0, 0))]
    return pl.pallas_call(
        functools.partial(_proj_kernel, v_key_tile=v_key_tile),
        grid=(n // tm,),
        in_specs=[
            pl.BlockSpec((tm, D_MODEL), row),
            pl.BlockSpec((tm, LANES), tab), pl.BlockSpec((tm, LANES), tab), pl.BlockSpec((tm, LANES), tab),
            pl.BlockSpec((tm, LANES), tab), pl.BlockSpec((tm, LANES), tab),
            _const_spec((1, D_MODEL)), _const_spec((D_MODEL, D_IN_EXT)),
            _const_spec((1, Q_LORA)), _const_spec((Q_LORA, D_HEADS_PAD)),
            _const_spec((1, KV_LORA)), _const_spec((KV_LORA, D_HEADS_PAD)),
            _const_spec((D_HEADS_PAD, KV_LORA)),
        ],
        out_specs=tuple(out_specs),
        out_shape=tuple(out_shape),
        compiler_params=pltpu.CompilerParams(dimension_semantics=("parallel",),
                                             vmem_limit_bytes=VMEM_LIMIT),
        name="proj",
    )(x2d, cosq, sinq_up, sinq_dn, cosk, sink, w["g_mix"], w["w_in_ext"], w["g_q_a"], w["w_q_pad"], w["g_kv_a"],
      w["w_k_pad"], w["w_v_t"])


def _attn_prompt_kernel(q_ref, k_ref, vt_ref, o_ref, s00, s01, s10, s11, mx_s, *, tq):
    i = pl.program_id(2)
    key_chunk = lax.broadcasted_iota(jnp.int32, (tq, tq), 0) // CHUNK
    query_chunk = lax.broadcasted_iota(jnp.int32, (tq, tq), 1) // CHUNK
    diag_mask = key_chunk <= query_chunk
    heads = [slice(h * HEAD_PAD, (h + 1) * HEAD_PAD) for h in range(2)]
    slots = ((s00, s01), (s10, s11))
    n_sub = ATTN_SUB
    ts = tq // n_sub

    def scores(j, slot):
        off = pl.multiple_of(j * tq, tq)
        for h, (sl, buf) in enumerate(zip(heads, slots[slot])):
            st = lax.dot_general(k_ref[pl.ds(off, tq), sl], q_ref[:, sl], (((1,), (1,)), ((), ())),
                                 preferred_element_type=F32)
            buf[...] = st
            for s in range(n_sub):
                mx_s[slot, h, s] = jnp.max(st[s * ts:(s + 1) * ts], axis=0, keepdims=True)

    def consume(j, slot, state, mask):
        for s in range(n_sub):
            rows = slice(s * ts, (s + 1) * ts)
            lo = s * ts if mask is not None else 0
            new_state = []
            for h, (sl, buf, (m, acc)) in enumerate(zip(heads, slots[slot], state)):
                st = buf[rows, lo:]
                if mask is not None:
                    st = jnp.where(mask[rows, lo:], st, NEG_INF)
                    m_new = jnp.maximum(m[:, lo:], jnp.max(st, axis=0, keepdims=True))
                else:
                    m_new = jnp.maximum(m, mx_s[slot, h, s])
                alpha = jnp.exp2(m[:, lo:] - m_new)
                p = jnp.exp2(st - m_new).astype(BF16)
                acc_new = acc[:, lo:] * alpha + jnp.dot(vt_ref[j, sl, rows], p, preferred_element_type=F32)
                if lo:
                    m_new = jnp.concatenate([m[:, :lo], m_new], axis=1)
                    acc_new = jnp.concatenate([acc[:, :lo], acc_new], axis=1)
                new_state.append((m_new, acc_new))
            state = tuple(new_state)
        return state

    def pair(p, state):
        scores(2 * p + 1, 1)
        state = consume(2 * p, 0, state, None)
        scores(2 * p + 2, 0)
        return consume(2 * p + 1, 1, state, None)

    init = tuple((jnp.full((1, tq), NEG_INF, F32), jnp.zeros((HEAD_PAD, tq), F32)) for _ in heads)
    scores(0, 0)
    state = lax.fori_loop(0, i // 2, pair, init)

    def even_tail(state):
        return consume(i, 0, state, diag_mask)

    def odd_tail(state):
        scores(i, 1)
        state = consume(i - 1, 0, state, None)
        return consume(i, 1, state, diag_mask)

    state = lax.cond(i % 2 == 0, even_tail, odd_tail, state)
    o_t = jnp.concatenate([acc[:V_DIM] / acc[V_DIM:V_DIM + 1] for _, acc in state], axis=0)
    o_ref[...] = o_t.T


def _attn_prompt(q, kpad, vt, tq):
    b, _, t, _ = q.shape
    assert tq % CHUNK == 0 and t % tq == 0 and vt.shape == (b, t // tq, D_HEADS_PAD, tq)
    return pl.pallas_call(
        functools.partial(_attn_prompt_kernel, tq=tq),
        grid=(b, N_HEADS // 2, t // tq),
        in_specs=[pl.BlockSpec((None, None, tq, 2 * HEAD_PAD), lambda bi, hp, i: (bi, hp, i, 0)),
                  pl.BlockSpec((None, None, t, 2 * HEAD_PAD), lambda bi, hp, i: (bi, hp, 0, 0)),
                  pl.BlockSpec((None, t // tq, 2 * HEAD_PAD, tq), lambda bi, hp, i: (bi, 0, hp, 0))],
        out_specs=pl.BlockSpec((None, tq, 2 * V_DIM), lambda bi, hp, i: (bi, i, hp)),
        out_shape=jax.ShapeDtypeStruct((b, t, D_ATTN), F32),
        scratch_shapes=[pltpu.VMEM((tq, tq), F32)] * 4 + [pltpu.VMEM((2, 2, ATTN_SUB, 1, tq), F32)],
        compiler_params=pltpu.CompilerParams(
            dimension_semantics=("parallel", "parallel", "arbitrary"), vmem_limit_bytes=VMEM_LIMIT),
        name="attn_prompt",
    )(q, kpad, vt)


D_KCAT = KV_LORA + LANES


def _attn_sample_kernel(q_ref, plat_ref, pkr_ref, nlat_ref, nkr_ref, wabs_ref, wv_ref, o_ref, s0, s1, sn, *, tk):
    tq = q_ref.shape[0]
    n_past = plat_ref.shape[0] // tk
    qcat = jnp.concatenate(
        [jnp.dot(q_ref[:, h * HEAD_PAD:(h + 1) * HEAD_PAD], wabs_ref[h], preferred_element_type=F32)
         for h in range(N_HEADS)], axis=0).astype(BF16)
    tiles = [(plat_ref.at[pl.ds(j * tk, tk)], pkr_ref.at[pl.ds(j * tk, tk)], (s0, s1)[j % 2])
             for j in range(n_past)] + [(nlat_ref, nkr_ref, sn)]

    def scores(lat_ref, kr_ref, buf):
        kr = kr_ref[...].astype(BF16)
        kcat = jnp.concatenate([lat_ref[...].astype(BF16), kr,
                                jnp.zeros((kr.shape[0], LANES - QK_ROPE), BF16)], axis=1)
        buf[...] = lax.dot_general(qcat, kcat, (((1,), (1,)), ((), ())), preferred_element_type=F32)

    def consume(lat_ref, buf, carry):
        keys = buf.shape[1]
        ts = min(keys, tk // ATTN_SUB)
        for c in range(keys // ts):
            m, l, acc = carry
            s = buf[:, c * ts:(c + 1) * ts]
            m_new = jnp.maximum(m, jnp.max(s, axis=-1, keepdims=True))
            alpha = jnp.exp2(m - m_new)
            p = jnp.exp2(s - m_new)
            l = l * alpha + jnp.sum(p, axis=-1, keepdims=True)
            acc = acc * alpha + jnp.dot(p.astype(BF16), lat_ref[c * ts:(c + 1) * ts, :].astype(BF16),
                                        preferred_element_type=F32)
            carry = (m_new, l, acc)
        return carry

    rows = N_HEADS * tq
    carry = (jnp.full((rows, 1), NEG_INF, F32), jnp.zeros((rows, 1), F32), jnp.zeros((rows, KV_LORA), F32))
    scores(*tiles[0])
    for idx, (lat_ref, _, buf) in enumerate(tiles):
        if idx + 1 < len(tiles):
            scores(*tiles[idx + 1])
        carry = consume(lat_ref, buf, carry)
    _, l, acc = carry
    o_lat = (acc / l).astype(BF16)
    out = jnp.zeros((tq, D_ATTN), F32)
    for h in range(N_HEADS):
        out = out + jnp.dot(o_lat[h * tq:(h + 1) * tq], wv_ref[h], preferred_element_type=F32)
    o_ref[...] = out


def _attn_sample(q, past_lat, past_kr, new_lat, new_kr, w, tk):
    b, tq, _ = q.shape
    past = past_lat.shape[1]
    assert past % tk == 0
    blk = lambda bi: (bi, 0, 0)
    return pl.pallas_call(
        functools.partial(_attn_sample_kernel, tk=tk),
        grid=(b,),
        in_specs=[pl.BlockSpec((None, tq, D_HEADS_PAD), blk),
                  pl.BlockSpec((None, past, KV_LORA), blk), pl.BlockSpec((None, past, QK_ROPE), blk),
                  pl.BlockSpec((None, tq, KV_LORA), blk), pl.BlockSpec((None, tq, QK_ROPE), blk),
                  _const_spec((N_HEADS, HEAD_PAD, D_KCAT)), _const_spec((N_HEADS, KV_LORA, D_ATTN))],
        out_specs=pl.BlockSpec((None, tq, D_ATTN), blk),
        out_shape=jax.ShapeDtypeStruct((b, tq, D_ATTN), F32),
        scratch_shapes=[pltpu.VMEM((N_HEADS * tq, tk), F32), pltpu.VMEM((N_HEADS * tq, tk), F32),
                        pltpu.VMEM((N_HEADS * tq, tq), F32)],
        compiler_params=pltpu.CompilerParams(dimension_semantics=("parallel",),
                                             vmem_limit_bytes=VMEM_LIMIT),
        name="attn_sample",
    )(q, past_lat, past_kr, new_lat, new_kr, w["w_q_abs"], w["w_v_heads"])


SSM_CH = LANES
SSM_ST = SSM_CH // SSM_GROUP * SSM_STATE
N_SSM_CHUNKS = D_SSM // SSM_CH


def _ssm_kernel(u_ref, h0_ref, lre_ref, lim_ref, bc_ref, cc_ref, d_ref, wglu_ref, gssm_ref,
                y_ref, hout_ref, x_s, h_s, hc_s, *, tt):
    ti = pl.program_id(1)

    @pl.when(ti == 0)
    def _():
        hc_s[...] = h0_ref[...]

    rows = tt * SUBLANES
    u = jnp.swapaxes(u_ref[...], 0, 1).reshape(rows, D_SSM)
    ub = u.astype(BF16)
    chunks = []
    for c in range(N_SSM_CHUNKS):
        re = slice(c * SSM_ST, (c + 1) * SSM_ST)
        im = slice(N_STATE + c * SSM_ST, N_STATE + (c + 1) * SSM_ST)
        chunks.append((re, im))
        xc = jnp.dot(ub[:, c * SSM_CH:(c + 1) * SSM_CH], bc_ref[c], preferred_element_type=F32)
        x_s[:, :, re] = xc[:, :SSM_ST].reshape(tt, SUBLANES, SSM_ST)
        x_s[:, :, im] = xc[:, SSM_ST:].reshape(tt, SUBLANES, SSM_ST)

    for re, im in chunks:
        lr = jnp.broadcast_to(lre_ref[:, re], (SUBLANES, SSM_ST))
        li = jnp.broadcast_to(lim_ref[:, re], (SUBLANES, SSM_ST))

        def step(t, carry, re=re, im=im, lr=lr, li=li):
            hr, hi = carry
            nhr = lr * hr - li * hi + x_s[t, :, re]
            nhi = lr * hi + li * hr + x_s[t, :, im]
            h_s[t, :, re] = nhr
            h_s[t, :, im] = nhi
            return nhr, nhi

        hr, hi = lax.fori_loop(0, tt, step, (hc_s[:, re], hc_s[:, im]), unroll=True)
        hc_s[:, re] = hr
        hc_s[:, im] = hi

    ys = []
    for c, (re, im) in enumerate(chunks):
        hcat = jnp.concatenate([h_s[:, :, re], h_s[:, :, im]], axis=-1).reshape(rows, 2 * SSM_ST)
        ys.append(jnp.dot(hcat.astype(BF16), cc_ref[c], preferred_element_type=F32))
    y = jnp.concatenate(ys, axis=1) + d_ref[...] * u
    y = jax.nn.gelu(y, approximate=True)
    z = jnp.dot(y.astype(BF16), wglu_ref[...], preferred_element_type=F32)
    y = y * (1.0 / (1.0 + jnp.exp(-z)))
    y_ref[...] = jnp.swapaxes(_rms(y, gssm_ref[...]).reshape(tt, SUBLANES, D_SSM), 0, 1)

    @pl.when(ti == pl.num_programs(1) - 1)
    def _():
        hout_ref[...] = hc_s[...]


def _ssm(u, h0, w, tt):
    nb, t, _ = u.shape
    assert nb % SUBLANES == 0 and t % tt == 0
    return pl.pallas_call(
        functools.partial(_ssm_kernel, tt=tt),
        grid=(nb // SUBLANES, t // tt),
        in_specs=[pl.BlockSpec((SUBLANES, tt, D_SSM), lambda bg, ti: (bg, ti, 0)),
                  pl.BlockSpec((SUBLANES, 2 * N_STATE), lambda bg, ti: (bg, 0)),
                  _const_spec((1, N_STATE)), _const_spec((1, N_STATE)),
                  _const_spec((N_SSM_CHUNKS, SSM_CH, 2 * SSM_ST)), _const_spec((N_SSM_CHUNKS, 2 * SSM_ST, SSM_CH)),
                  _const_spec((1, D_SSM)), _const_spec((D_SSM, D_SSM)), _const_spec((1, D_SSM))],
        out_specs=(pl.BlockSpec((SUBLANES, tt, D_SSM), lambda bg, ti: (bg, ti, 0)),
                   pl.BlockSpec((SUBLANES, 2 * N_STATE), lambda bg, ti: (bg, 0))),
        out_shape=(jax.ShapeDtypeStruct((nb, t, D_SSM), F32),
                   jax.ShapeDtypeStruct((nb, 2 * N_STATE), F32)),
        scratch_shapes=[pltpu.VMEM((tt, SUBLANES, 2 * N_STATE), F32),
                        pltpu.VMEM((tt, SUBLANES, 2 * N_STATE), F32),
                        pltpu.VMEM((SUBLANES, 2 * N_STATE), F32)],
        compiler_params=pltpu.CompilerParams(dimension_semantics=("parallel", "arbitrary"),
                                             vmem_limit_bytes=VMEM_LIMIT),
        name="ssm",
    )(u, h0, w["lam_re"], w["lam_im"], w["b_chunks"], w["c_chunks"], w["d_skip"], w["w_glu"],
      w["g_ssm_out"])


def _post_kernel(x_ref, attn_ref, ssm_ref, gattn_ref, wouta_ref, wouts_ref, gmlp_ref, wup_ref, wdown_ref,
                 gfin_ref, y_ref):
    an = _rms(attn_ref[...], gattn_ref[...]).astype(BF16)
    mixed = (jnp.dot(an, wouta_ref[...], preferred_element_type=F32)
             + jnp.dot(ssm_ref[...].astype(BF16), wouts_ref[...], preferred_element_type=F32))
    h = x_ref[...] + mixed
    hn = _rms(h, gmlp_ref[...]).astype(BF16)
    acc = jnp.zeros(h.shape, F32)
    for c in range(D_FF // FF_CHUNK):
        ff = slice(c * FF_CHUNK, (c + 1) * FF_CHUNK)
        a = jnp.dot(hn, wup_ref[:, ff], preferred_element_type=F32)
        a = jnp.square(jnp.maximum(a, 0.0))
        acc = acc + jnp.dot(a.astype(BF16), wdown_ref[ff, :], preferred_element_type=F32)
    y_ref[...] = _rms(h + acc, gfin_ref[...])


def _post(x2d, attn2d, ssm2d, w, tm):
    n = x2d.shape[0]
    row = lambda i: (i, 0)
    return pl.pallas_call(
        _post_kernel,
        grid=(n // tm,),
        in_specs=[pl.BlockSpec((tm, D_MODEL), row), pl.BlockSpec((tm, D_ATTN), row),
                  pl.BlockSpec((tm, D_SSM), row),
                  _const_spec((1, D_ATTN)), _const_spec((D_ATTN, D_MODEL)), _const_spec((D_SSM, D_MODEL)),
                  _const_spec((1, D_MODEL)), _const_spec((D_MODEL, D_FF)), _const_spec((D_FF, D_MODEL)),
                  _const_spec((1, D_MODEL))],
        out_specs=pl.BlockSpec((tm, D_MODEL), row),
        out_shape=jax.ShapeDtypeStruct((n, D_MODEL), F32),
        compiler_params=pltpu.CompilerParams(dimension_semantics=("parallel",),
                                             vmem_limit_bytes=VMEM_LIMIT),
        name="post",
    )(x2d, attn2d, ssm2d, w["g_attn_out"], w["w_out_attn"], w["w_out_ssm"], w["g_mlp"], w["w_up"],
      w["w_down"], w["g_final"])


def _rope_tables(pos):
    t = pos.shape[0]
    inv_freq = ROPE_THETA ** (-(jnp.arange(ROPE_HALF, dtype=F32) * 2.0) / QK_ROPE)
    ang = pos.astype(F32)[:, None] * inv_freq[None, :]
    cc = jnp.tile(jnp.cos(ang), (1, 2))
    ss = jnp.tile(jnp.sin(ang), (1, 2))
    pad_hi = jnp.zeros((t, HEAD_PAD - QK_NOPE - QK_ROPE), F32)
    qs = SOFTMAX_SCALE * LOG2_E
    cosq = jnp.concatenate([jnp.full((t, QK_NOPE), qs, F32), cc * qs, pad_hi], axis=1)
    z_half = jnp.zeros((t, ROPE_HALF), F32)
    sin_h = jnp.sin(ang) * qs
    sinq_up = jnp.concatenate([jnp.zeros((t, QK_NOPE), F32), z_half, sin_h, pad_hi], axis=1)
    sinq_dn = jnp.concatenate([jnp.zeros((t, QK_NOPE), F32), -sin_h, z_half, pad_hi], axis=1)
    pad_k = jnp.zeros((t, LANES - QK_ROPE), F32)
    cosk = jnp.concatenate([cc, pad_k], axis=1)
    sink = jnp.concatenate([ss, pad_k], axis=1)
    return cosq, sinq_up, sinq_dn, cosk, sink


def _rot_cols(w_x1, w_x2):
    return -w_x2, w_x1


def _layer_weights(g_mix, w_in, g_q_a, w_q_up, g_kv_a, w_kv_up, a_re, a_im, log_step, b_re, b_im,
                   c_re, c_im, d_skip, w_glu, g_attn_out, g_ssm_out, w_out, g_mlp, w_up, w_down, g_final):
    w = {}
    w["g_mix"] = g_mix[None, :]
    w_cq, w_ckv = w_in[:, :Q_LORA], w_in[:, Q_LORA:Q_LORA + KV_LORA]
    w_kpe = w_in[:, Q_LORA + KV_LORA:Q_LORA + KV_LORA + QK_ROPE]
    w_u = w_in[:, Q_LORA + KV_LORA + QK_ROPE:]
    rot1, rot2 = _rot_cols(w_kpe[:, :ROPE_HALF], w_kpe[:, ROPE_HALF:])
    zk = jnp.zeros((D_MODEL, LANES - QK_ROPE), F32)
    w["w_in_ext"] = jnp.concatenate([w_cq, w_ckv, w_u, w_kpe, zk, rot1, rot2, zk], axis=1).astype(BF16)

    w["g_q_a"] = g_q_a[None, :]
    wq = w_q_up.reshape(Q_LORA, N_HEADS, QK_NOPE + QK_ROPE)
    nope, r1, r2 = wq[:, :, :QK_NOPE], wq[:, :, QK_NOPE:QK_NOPE + ROPE_HALF], wq[:, :, QK_NOPE + ROPE_HALF:]
    zq = jnp.zeros((Q_LORA, N_HEADS, HEAD_PAD - QK_NOPE - QK_ROPE), F32)
    w["w_q_pad"] = jnp.concatenate([nope, r1, r2, zq], axis=2).reshape(Q_LORA, D_HEADS_PAD).astype(BF16)

    w["g_kv_a"] = g_kv_a[None, :]
    wkv = w_kv_up.reshape(KV_LORA, N_HEADS, QK_NOPE + V_DIM)
    zkv = jnp.zeros((KV_LORA, N_HEADS, HEAD_PAD - QK_NOPE), F32)
    wk_pad = jnp.concatenate([wkv[:, :, :QK_NOPE], zkv], axis=2).reshape(KV_LORA, D_HEADS_PAD)
    wv_pad = jnp.concatenate([wkv[:, :, QK_NOPE:], zkv], axis=2).reshape(KV_LORA, D_HEADS_PAD)
    w["w_k_pad"] = wk_pad.astype(BF16)
    w["w_v_t"] = wv_pad.T.astype(BF16)
    wk_t = jnp.transpose(wkv[:, :, :QK_NOPE], (1, 2, 0))
    q_abs = jnp.zeros((N_HEADS, HEAD_PAD, D_KCAT), F32).at[:, :QK_NOPE, :KV_LORA].set(wk_t)
    q_abs = q_abs.at[:, QK_NOPE + jnp.arange(QK_ROPE), KV_LORA + jnp.arange(QK_ROPE)].set(1.0)
    w["w_q_abs"] = q_abs.astype(BF16)
    wv_h = jnp.transpose(wkv[:, :, QK_NOPE:], (1, 0, 2))
    w["w_v_heads"] = jnp.einsum("hce,hg->hcge", wv_h, jnp.eye(N_HEADS, dtype=F32)).reshape(
        N_HEADS, KV_LORA, D_ATTN).astype(BF16)

    lam_re, lam_im, bb_re, bb_im = _ssm_prep(a_re, a_im, log_step, b_re, b_im)
    w["lam_re"] = lam_re.reshape(1, N_STATE)
    w["lam_im"] = lam_im.reshape(1, N_STATE)
    gpc = SSM_CH // SSM_GROUP
    eye = jnp.eye(gpc, dtype=F32)
    by_chunk = lambda m: m.reshape(N_SSM_CHUNKS, gpc, SSM_GROUP, SSM_STATE)
    blk = lambda m: jnp.einsum("cgpn,gh->cgphn", by_chunk(m), eye).reshape(N_SSM_CHUNKS, SSM_CH, SSM_ST)
    w["b_chunks"] = jnp.concatenate([blk(bb_re), blk(bb_im)], axis=2).astype(BF16)
    blk_t = lambda m: jnp.einsum("cgpn,gh->cgnhp", by_chunk(m), eye).reshape(N_SSM_CHUNKS, SSM_ST, SSM_CH)
    w["c_chunks"] = jnp.concatenate([blk_t(c_re), blk_t(-c_im)], axis=1).astype(BF16)
    w["d_skip"] = d_skip[None, :]
    w["w_glu"] = w_glu.astype(BF16)
    w["g_ssm_out"] = g_ssm_out[None, :]

    w["g_attn_out"] = g_attn_out[None, :]
    w["w_out_attn"] = w_out[:D_ATTN].astype(BF16)
    w["w_out_ssm"] = w_out[D_ATTN:].astype(BF16)
    w["g_mlp"] = g_mlp[None, :]
    w["w_up"] = w_up.astype(BF16)
    w["w_down"] = w_down.astype(BF16)
    w["g_final"] = g_final[None, :]
    return w


def _pack_state(h_re, h_im):
    nb = h_re.shape[0]
    return jnp.concatenate([h_re.reshape(nb, N_STATE), h_im.reshape(nb, N_STATE)], axis=1)


def _unpack_state(h):
    nb = h.shape[0]
    return (h[:, :N_STATE].reshape(nb, N_GROUPS, SSM_STATE), h[:, N_STATE:].reshape(nb, N_GROUPS, SSM_STATE))


def _branch(x, pos, past, h0, w, *, proj_tm, post_tm, ssm_tt):
    b, t, _ = x.shape
    n = b * t
    x2d = x.reshape(n, D_MODEL)
    tabs = _rope_tables(pos)
    if t < proj_tm:
        assert proj_tm % t == 0
        tabs = tuple(jnp.tile(a, (proj_tm // t, 1)) for a in tabs)
    else:
        assert t % proj_tm == 0
    if past is None:
        tq = min(ATTN_TQ, t)
        q, lat, kr, u, kpad, vt = _proj(x2d, tabs, w, proj_tm, t, tq)
        attn = _attn_prompt(q, kpad, vt, tq)
    else:
        q, lat, kr, u = _proj(x2d, tabs, w, proj_tm, t, None)
        past_lat, past_kr = past
        plen = past_lat.shape[1]
        assert plen % CHUNK == 0 and t <= CHUNK
        attn = _attn_sample(q.reshape(b, t, D_HEADS_PAD), past_lat, past_kr, lat.reshape(b, t, KV_LORA),
                            kr.reshape(b, t, QK_ROPE), w, min(ATTN_TK_SAMPLE, plen))
    y_ssm, h_fin = _ssm(u.reshape(b, t, D_SSM), h0, w, min(ssm_tt, t))
    ssm2d = y_ssm.reshape(n, D_SSM)
    y = _post(x2d, attn.reshape(n, D_ATTN), ssm2d, w, post_tm)
    h_re, h_im = _unpack_state(h_fin)
    return (y.reshape(b, t, D_MODEL), lat.reshape(1, b, t, KV_LORA), kr.reshape(1, b, t, QK_ROPE),
            h_re[None], h_im[None])


def kernel(x_prompt, x_sample, cache_kv_latent, cache_k_rope, state_ssm_re, state_ssm_im, g_mix, w_in, g_q_a,
           w_q_up, g_kv_a, w_kv_up, a_re, a_im, log_step, b_re, b_im, c_re, c_im, d_skip, w_glu, g_attn_out,
           g_ssm_out, w_out, g_mlp, w_up, w_down, g_final):
    assert g_mix.shape[0] == 1, "single-layer trunk"
    w = _layer_weights(g_mix[0], w_in[0], g_q_a[0], w_q_up[0], g_kv_a[0], w_kv_up[0], a_re[0], a_im[0],
                       log_step[0], b_re[0], b_im[0], c_re[0], c_im[0], d_skip[0], w_glu[0], g_attn_out[0],
                       g_ssm_out[0], w_out[0], g_mlp[0], w_up[0], w_down[0], g_final)
    tiles = dict(proj_tm=PROJ_TM, post_tm=POST_TM, ssm_tt=SSM_TT)

    bp, tp, _ = x_prompt.shape
    pos_p = jnp.arange(tp, dtype=jnp.int32)
    h0p = jnp.zeros((bp, 2 * N_STATE), F32)
    y_p, lat_p, kr_p, hr_p, hi_p = _branch(x_prompt, pos_p, None, h0p, w, **tiles)

    bs, ts, _ = x_sample.shape
    plen = cache_kv_latent.shape[2]
    pos_s = plen + jnp.arange(ts, dtype=jnp.int32)
    h0s = _pack_state(state_ssm_re[0], state_ssm_im[0])
    y_s, lat_s, kr_s, hr_s, hi_s = _branch(x_sample, pos_s, (cache_kv_latent[0], cache_k_rope[0]), h0s, w, **tiles)
    return (y_p, y_s, lat_p, kr_p, hr_p, hi_p, lat_s, kr_s, hr_s, hi_s)
```
